```python
import math
import jax
import jax.numpy as jnp
from jax import lax
import numpy as np


D_MODEL = 2048
BATCH = 2
SEQ = 16384
DEPTH = 1

SSM_EXPAND = 2
SSM_D_INNER = SSM_EXPAND * D_MODEL
SSM_HEAD_DIM = 64
SSM_N_HEADS = SSM_D_INNER // SSM_HEAD_DIM
SSM_N_GROUPS = 8
SSM_D_STATE = 128
SSM_CONV = 4
SSM_CHUNK = 128
SSM_CONV_DIM = SSM_D_INNER + 2 * SSM_N_GROUPS * SSM_D_STATE
ATT_N_HEADS = 16
ATT_N_KV_HEADS = 4
ATT_HEAD_DIM = 128
IDX_N_HEADS = 8
IDX_HEAD_DIM = 64
IDX_TOPK_MAX = 256
Q_BLOCK = 128
ROPE_THETA = 10000.0
N_EXPERTS = 32
TOP_K = 4
D_EXPERT = D_MODEL
SWIGLU_ALPHA = 1.702
SWIGLU_LIMIT = 7.0
MOE_BLOCK = 512
NORM_EPS = 1e-6

IN_SIZES = (
    SSM_D_INNER,
    SSM_CONV_DIM,
    SSM_N_HEADS,
    ATT_N_HEADS * ATT_HEAD_DIM,
    ATT_N_KV_HEADS * ATT_HEAD_DIM,
    ATT_N_KV_HEADS * ATT_HEAD_DIM,
    IDX_N_HEADS * IDX_HEAD_DIM,
    IDX_HEAD_DIM,
    IDX_N_HEADS,
    D_MODEL,
    D_MODEL,
)
D_IN_PROJ = sum(IN_SIZES)

kernel_name = 'hybrid_ssd_dsa_moe_block'


def rms_norm(x, g):
    xf = x.astype(jnp.float32)
    y = xf * lax.rsqrt(jnp.mean(xf * xf, axis=-1, keepdims=True) + NORM_EPS)
    return (y * g.astype(jnp.float32)).astype(x.dtype)


def rope_tables(positions, dim):
    inv = ROPE_THETA ** (-jnp.arange(0, dim, 2, dtype=jnp.float32) / dim)
    ang = positions.astype(jnp.float32)[..., None] * inv
    return jnp.cos(ang), jnp.sin(ang)


def apply_rope(x, cos, sin):
    xf = x.astype(jnp.float32)
    half = x.shape[-1] // 2
    x1, x2 = xf[..., :half], xf[..., half:]
    c = cos[:, :, None, :]
    s = sin[:, :, None, :]
    return jnp.concatenate([x1 * c - x2 * s, x2 * c + x1 * s], axis=-1).astype(x.dtype)


def causal_depthwise_conv(u, w, b):
    ch = u.shape[-1]
    y = lax.conv_general_dilated(u, w[:, None, :].astype(u.dtype), window_strides=(1,),
                                 padding=[(SSM_CONV - 1, 0)],
                                 dimension_numbers=('NWC', 'WIO', 'NWC'),
                                 feature_group_count=ch)
    return y + b.astype(u.dtype)


def ssd_chunked(xh, dt, A, bm, cm, d_skip):
    bsz, S, H, P = xh.shape
    G, N = bm.shape[2], bm.shape[3]
    R = H // G
    L = SSM_CHUNK
    nc = S // L
    xf = xh.astype(jnp.float32)
    xg = xf.reshape(bsz, S, G, R, P)
    dtg = dt.reshape(bsz, S, G, R)
    Ag = A.reshape(G, R)

    def chunks(t):
        return jnp.moveaxis(t.reshape(bsz, nc, L, *t.shape[2:]), 1, 0)

    causal = jnp.tril(jnp.ones((L, L), dtype=bool))[None, :, :, None, None]

    def step(state, inp):
        x, d, b, c = inp
        xdt = x * d[..., None]
        acum = jnp.cumsum(d * Ag, axis=1)
        seg = acum[:, :, None] - acum[:, None, :]
        decay = jnp.exp(jnp.where(causal, seg, -jnp.inf))
        cb = jnp.einsum('btgn,bsgn->btsg', c, b)
        y = jnp.einsum('btsg,btsgr,bsgrp->btgrp', cb, decay, xdt)
        y = y + jnp.einsum('btgn,bgrpn->btgrp', c, state) * jnp.exp(acum)[..., None]
        to_end = jnp.exp(acum[:, -1:] - acum)
        state = state * jnp.exp(acum[:, -1])[..., None, None] + jnp.einsum('bsgn,bsgr,bsgrp->bgrpn', b, to_end, xdt)
        return state, y

    state0 = jnp.zeros((bsz, G, R, P, N), jnp.float32)
    _, ys = lax.scan(step, state0, (chunks(xg), chunks(dtg), chunks(bm.astype(jnp.float32)), chunks(cm.astype(jnp.float32))))
    y = jnp.moveaxis(ys, 0, 1).reshape(bsz, S, H, P)
    return y + xf * d_skip.astype(jnp.float32)[:, None]


def ssm_branch(z, xbc, dt_raw, conv_w, conv_b, dt_bias, a_log, d_skip, norm_g):
    bsz, S, _ = xbc.shape
    xbc = jax.nn.silu(causal_depthwise_conv(xbc, conv_w, conv_b))
    gn = SSM_N_GROUPS * SSM_D_STATE
    xs = xbc[..., :SSM_D_INNER].reshape(bsz, S, SSM_N_HEADS, SSM_HEAD_DIM)
    bm = xbc[..., SSM_D_INNER:SSM_D_INNER + gn].reshape(bsz, S, SSM_N_GROUPS, SSM_D_STATE)
    cm = xbc[..., SSM_D_INNER + gn:].reshape(bsz, S, SSM_N_GROUPS, SSM_D_STATE)
    dt = jax.nn.softplus(dt_raw.astype(jnp.float32) + dt_bias.astype(jnp.float32))
    A = -jnp.exp(a_log.astype(jnp.float32))
    y = ssd_chunked(xs, dt, A, bm, cm, d_skip).reshape(bsz, S, SSM_D_INNER)
    yg = (y * jax.nn.silu(z.astype(jnp.float32))).reshape(bsz, S, SSM_N_GROUPS, SSM_D_INNER // SSM_N_GROUPS)
    yg = yg * lax.rsqrt(jnp.mean(yg * yg, axis=-1, keepdims=True) + NORM_EPS)
    yg = yg.reshape(bsz, S, SSM_D_INNER) * norm_g.astype(jnp.float32)
    return yg.astype(z.dtype)


def dsa_branch(q, k, v, q_idx, k_idx, w_idx):
    bsz, S = q.shape[0], q.shape[1]
    nb = S // Q_BLOCK
    topk = min(IDX_TOPK_MAX, S // 4)
    R = ATT_N_HEADS // ATT_N_KV_HEADS
    k_idx_f = k_idx.astype(jnp.float32)
    key_pos = jnp.arange(S, dtype=jnp.int32)
    idx_scale = IDX_HEAD_DIM ** -0.5
    w_scale = IDX_N_HEADS ** -0.5
    att_scale = ATT_HEAD_DIM ** -0.5
    gather = jax.vmap(lambda a, i: a[i])

    def block(inp):
        qb, qib, wib, start = inp
        qpos = start + jnp.arange(Q_BLOCK, dtype=jnp.int32)
        causal = key_pos[None, None, :] <= qpos[None, :, None]
        logits = jnp.einsum('bqhd,bsd->bqhs', qib.astype(jnp.float32), k_idx_f) * idx_scale
        score = jnp.einsum('bqhs,bqh->bqs', jax.nn.relu(logits), wib.astype(jnp.float32) * w_scale)
        score = jnp.where(causal, score, -jnp.inf)
        _, sel = lax.top_k(score, topk)
        valid = sel <= qpos[None, :, None]
        ks = gather(k, sel)
        vs = gather(v, sel)
        qg = qb.reshape(bsz, Q_BLOCK, ATT_N_KV_HEADS, R, ATT_HEAD_DIM)
        s = jnp.einsum('bqgrd,bqkgd->bqgrk', qg.astype(jnp.float32), ks.astype(jnp.float32)) * att_scale
        s = jnp.where(valid[:, :, None, None, :], s, -jnp.inf)
        p = jax.nn.softmax(s, axis=-1).astype(v.dtype)
        o = jnp.einsum('bqgrk,bqkgd->bqgrd', p, vs)
        return o.reshape(bsz, Q_BLOCK, ATT_N_HEADS * ATT_HEAD_DIM)

    def blocks(t):
        return jnp.moveaxis(t.reshape(bsz, nb, Q_BLOCK, *t.shape[2:]), 1, 0)

    starts = jnp.arange(nb, dtype=jnp.int32) * Q_BLOCK
    out = lax.map(block, (blocks(q), blocks(q_idx), blocks(w_idx), starts))
    return jnp.moveaxis(out, 0, 1).reshape(bsz, S, ATT_N_HEADS * ATT_HEAD_DIM)


def hybrid_mixer(h, cos_a, sin_a, cos_i, sin_i, w_in, conv_w, conv_b, dt_bias, a_log, d_skip,
                 ssm_norm_g, w_out_ssm, w_out_att, w_o):
    bsz, S, _ = h.shape
    proj = h @ w_in.astype(h.dtype)
    points = [int(p) for p in np.cumsum(IN_SIZES)[:-1]]
    z, xbc, dt_raw, q, k, v, qi, ki, wi, g_ssm, g_att = jnp.split(proj, points, axis=-1)
    y_ssm = ssm_branch(z, xbc, dt_raw, conv_w, conv_b, dt_bias, a_log, d_skip, ssm_norm_g)
    q = apply_rope(q.reshape(bsz, S, ATT_N_HEADS, ATT_HEAD_DIM), cos_a, sin_a)
    k = apply_rope(k.reshape(bsz, S, ATT_N_KV_HEADS, ATT_HEAD_DIM), cos_a, sin_a)
    v = v.reshape(bsz, S, ATT_N_KV_HEADS, ATT_HEAD_DIM)
    qi = apply_rope(qi.reshape(bsz, S, IDX_N_HEADS, IDX_HEAD_DIM), cos_i, sin_i)
    ki = apply_rope(ki[:, :, None, :], cos_i, sin_i)[:, :, 0, :]
    o_att = dsa_branch(q, k, v, qi, ki, wi)
    merged = (jax.nn.sigmoid(g_ssm) * (y_ssm @ w_out_ssm.astype(h.dtype))
              + jax.nn.sigmoid(g_att) * (o_att @ w_out_att.astype(h.dtype)))
    return merged @ w_o.astype(h.dtype)


def moe_ffn(h, router_w, router_b, w_gu, b_gu, w_dn, b_dn):
    bsz, S, D = h.shape
    n_tok = bsz * S
    t = h.reshape(n_tok, D)
    logits = t.astype(jnp.float32) @ router_w.astype(jnp.float32) + router_b.astype(jnp.float32)
    top_val, top_idx = lax.top_k(logits, TOP_K)
    gates = jax.nn.softmax(top_val, axis=-1)
    n_asg = n_tok * TOP_K
    e_flat = top_idx.reshape(-1).astype(jnp.int32)
    tok_flat = jnp.arange(n_asg, dtype=jnp.int32) // TOP_K
    g_flat = gates.reshape(-1)
    order = jnp.argsort(e_flat)
    e_sorted = e_flat[order]
    counts = jnp.bincount(e_flat, length=N_EXPERTS).astype(jnp.int32)
    padded = (counts + MOE_BLOCK - 1) // MOE_BLOCK * MOE_BLOCK
    ends_padded = jnp.cumsum(padded)
    start_sorted = jnp.cumsum(counts) - counts
    start_padded = ends_padded - padded
    dest = start_padded[e_sorted] + jnp.arange(n_asg, dtype=jnp.int32) - start_sorted[e_sorted]
    n_blocks = -(-(n_asg + N_EXPERTS * (MOE_BLOCK - 1)) // MOE_BLOCK)
    n_rows = n_blocks * MOE_BLOCK
    row_tok = jnp.zeros((n_rows,), jnp.int32).at[dest].set(tok_flat[order])
    row_gate = jnp.zeros((n_rows,), jnp.float32).at[dest].set(g_flat[order])
    block_expert = jnp.minimum(
        jnp.searchsorted(ends_padded, jnp.arange(n_blocks, dtype=jnp.int32) * MOE_BLOCK, side='right'),
        N_EXPERTS - 1)

    def body(acc, inp):
        e, toks, g = inp
        xb = t[toks]
        gu = xb @ w_gu[e].astype(xb.dtype) + b_gu[e].astype(xb.dtype)
        gate = jnp.minimum(gu[:, :D_EXPERT], SWIGLU_LIMIT)
        up = jnp.clip(gu[:, D_EXPERT:], -SWIGLU_LIMIT, SWIGLU_LIMIT)
        act = (up + 1) * (gate * jax.nn.sigmoid(SWIGLU_ALPHA * gate))
        yb = act @ w_dn[e].astype(xb.dtype) + b_dn[e].astype(xb.dtype)
        return acc.at[toks].add(yb.astype(jnp.float32) * g[:, None]), None

    acc, _ = lax.scan(body, jnp.zeros((n_tok, D), jnp.float32),
                      (block_expert, row_tok.reshape(n_blocks, MOE_BLOCK), row_gate.reshape(n_blocks, MOE_BLOCK)))
    return acc.reshape(bsz, S, D).astype(h.dtype)


def setup_inputs(seed: int = 0) -> dict:
    key = jax.random.key(seed)
    ks = jax.random.split(key, 24)
    f32 = jnp.float32

    def nrm(k, shape, scale):
        return jax.random.normal(k, shape, f32) * scale

    x = nrm(ks[0], (BATCH, SEQ, D_MODEL), 1.0)
    c = nrm(ks[1], (BATCH, D_MODEL), 1.0)
    positions = jnp.broadcast_to(jnp.arange(SEQ, dtype=jnp.int32), (BATCH, SEQ))
    ada_w = nrm(ks[2], (DEPTH, D_MODEL, 6 * D_MODEL), D_MODEL ** -0.5)
    ada_b = nrm(ks[3], (DEPTH, 6 * D_MODEL), 0.02)
    norm_mix_g = 1.0 + nrm(ks[4], (DEPTH, D_MODEL), 0.02)
    w_in = nrm(ks[5], (DEPTH, D_MODEL, D_IN_PROJ), D_MODEL ** -0.5)
    conv_w = nrm(ks[6], (DEPTH, SSM_CONV, SSM_CONV_DIM), SSM_CONV ** -0.5)
    conv_b = nrm(ks[7], (DEPTH, SSM_CONV_DIM), 0.02)
    dt0 = jnp.exp(jax.random.uniform(ks[8], (DEPTH, SSM_N_HEADS), f32, math.log(1e-3), math.log(1e-1)))
    dt_bias = dt0 + jnp.log(-jnp.expm1(-dt0))
    a_log = jnp.log(jax.random.uniform(ks[9], (DEPTH, SSM_N_HEADS), f32, 1.0, 16.0))
    d_skip = 1.0 + nrm(ks[10], (DEPTH, SSM_N_HEADS), 0.1)
    ssm_norm_g = 1.0 + nrm(ks[11], (DEPTH, SSM_D_INNER), 0.02)
    w_out_ssm = nrm(ks[12], (DEPTH, SSM_D_INNER, D_MODEL), SSM_D_INNER ** -0.5)
    w_out_att = nrm(ks[13], (DEPTH, ATT_N_HEADS * ATT_HEAD_DIM, D_MODEL), (ATT_N_HEADS * ATT_HEAD_DIM) ** -0.5)
    w_o = nrm(ks[14], (DEPTH, D_MODEL, D_MODEL), D_MODEL ** -0.5)
    norm_ffn_g = 1.0 + nrm(ks[15], (DEPTH, D_MODEL), 0.02)
    router_w = nrm(ks[16], (DEPTH, D_MODEL, N_EXPERTS), D_MODEL ** -0.5)
    router_b = nrm(ks[17], (DEPTH, N_EXPERTS), 0.01)
    expert_w_gate_up = nrm(ks[18], (DEPTH, N_EXPERTS, D_MODEL, 2 * D_EXPERT), D_MODEL ** -0.5)
    expert_b_gate_up = nrm(ks[19], (DEPTH, N_EXPERTS, 2 * D_EXPERT), 0.02)
    expert_w_down = nrm(ks[20], (DEPTH, N_EXPERTS, D_EXPERT, D_MODEL), D_EXPERT ** -0.5)
    expert_b_down = nrm(ks[21], (DEPTH, N_EXPERTS, D_MODEL), 0.02)
    final_norm_g = 1.0 + nrm(ks[22], (D_MODEL,), 0.02)
    return {'x': x, 'c': c, 'positions': positions, 'ada_w': ada_w, 'ada_b': ada_b,
            'norm_mix_g': norm_mix_g, 'w_in': w_in, 'conv_w': conv_w, 'conv_b': conv_b,
            'dt_bias': dt_bias, 'a_log': a_log, 'd_skip': d_skip, 'ssm_norm_g': ssm_norm_g,
            'w_out_ssm': w_out_ssm, 'w_out_att': w_out_att, 'w_o': w_o, 'norm_ffn_g': norm_ffn_g,
            'router_w': router_w, 'router_b': router_b, 'expert_w_gate_up': expert_w_gate_up,
            'expert_b_gate_up': expert_b_gate_up, 'expert_w_down': expert_w_down,
            'expert_b_down': expert_b_down, 'final_norm_g': final_norm_g}


def reference(x, c, positions, ada_w, ada_b, norm_mix_g, w_in, conv_w, conv_b, dt_bias, a_log, d_skip,
              ssm_norm_g, w_out_ssm, w_out_att, w_o, norm_ffn_g, router_w, router_b, expert_w_gate_up,
              expert_b_gate_up, expert_w_down, expert_b_down, final_norm_g):
    cos_a, sin_a = rope_tables(positions, ATT_HEAD_DIM)
    cos_i, sin_i = rope_tables(positions, IDX_HEAD_DIM)
    c_act = jax.nn.silu(c.astype(jnp.float32))
    for l in range(DEPTH):
        mod = c_act @ ada_w[l].astype(jnp.float32) + ada_b[l].astype(jnp.float32)
        sh1, sc1, gt1, sh2, sc2, gt2 = [m[:, None, :].astype(x.dtype) for m in jnp.split(mod, 6, axis=-1)]
        h = rms_norm(x, norm_mix_g[l]) * (1 + sc1) + sh1
        x = x + gt1 * hybrid_mixer(h, cos_a, sin_a, cos_i, sin_i, w_in[l], conv_w[l], conv_b[l],
                                   dt_bias[l], a_log[l], d_skip[l], ssm_norm_g[l],
                                   w_out_ssm[l], w_out_att[l], w_o[l])
        h = rms_norm(x, norm_ffn_g[l]) * (1 + sc2) + sh2
        x = x + gt2 * moe_ffn(h, router_w[l], router_b[l], expert_w_gate_up[l], expert_b_gate_up[l],
                              expert_w_down[l], expert_b_down[l])
    return rms_norm(x, final_norm_g)
```

```python
import functools

import jax
import jax.numpy as jnp
import numpy as np
from jax import lax
from jax.experimental import pallas as pl
from jax.experimental.pallas import tpu as pltpu

F32 = jnp.float32
BF16 = jnp.bfloat16
I32 = jnp.int32

SSM_HEAD_DIM = 64
SSM_N_GROUPS = 8
SSM_D_STATE = 128
SSM_CONV = 4
SSM_CHUNK = 128
ATT_N_HEADS = 16
ATT_N_KV_HEADS = 4
ATT_HEAD_DIM = 128
IDX_N_HEADS = 8
IDX_HEAD_DIM = 64
IDX_TOPK_MAX = 256
ROPE_THETA = 10000.0
N_EXPERTS = 32
TOP_K = 4
SWIGLU_ALPHA = 1.702
SWIGLU_LIMIT = 7.0
NORM_EPS = 1e-6

INT_MIN = -2147483648
NEG_BIG = -1e30
VMEM_LIMIT = 56 * 1024 * 1024


def _cparams(sem):
    return pltpu.CompilerParams(dimension_semantics=sem, vmem_limit_bytes=VMEM_LIMIT)


def _split3(a):
    hi = a.astype(BF16)
    r1 = a - hi.astype(F32)
    mid = r1.astype(BF16)
    lo = (r1 - mid.astype(F32)).astype(BF16)
    return hi, mid, lo


def _dot_x3(a, b_bf16, dims=None):
    out = None
    for p in _split3(a):
        if dims is None:
            t = jnp.dot(p, b_bf16, preferred_element_type=F32)
        else:
            t = lax.dot_general(p, b_bf16, dims, preferred_element_type=F32)
        out = t if out is None else out + t
    return out


def _dot_hi(a, b):
    a0, a1, a2 = _split3(a)
    b0, b1, b2 = _split3(b)
    d = functools.partial(jnp.dot, preferred_element_type=F32)
    return (d(a0, b0) + (d(a0, b1) + d(a1, b0))
            + (d(a0, b2) + d(a2, b0) + d(a1, b1)))


def _sigmoid(x):
    return 1.0 / (1.0 + jnp.exp(-x))


def _silu(x):
    return x * _sigmoid(x)


def _adaln_kernel(c_ref, w_ref, b_ref, o_ref):
    c = c_ref[...]
    o_ref[...] = _dot_hi(_silu(c), w_ref[...]) + b_ref[...]


def _adaln(c, ada_w, ada_b):
    bsz, d = c.shape
    n = ada_w.shape[1]
    rows = 8
    cp = jnp.zeros((rows, d), F32).at[:bsz].set(c)
    tn = 1024
    out = pl.pallas_call(
        _adaln_kernel, name="adaln",
        grid=(n // tn,),
        in_specs=[pl.BlockSpec((rows, d), lambda j: (0, 0)),
                  pl.BlockSpec((d, tn), lambda j: (0, j)),
                  pl.BlockSpec((1, tn), lambda j: (0, j))],
        out_specs=pl.BlockSpec((rows, tn), lambda j: (0, j)),
        out_shape=jax.ShapeDtypeStruct((rows, n), F32),
        compiler_params=_cparams(("arbitrary",)),
    )(cp, ada_w, ada_b.reshape(1, n))
    return out[:bsz]


def _modnorm(x, g, sc, sh):
    ms = jnp.mean(x * x, axis=-1, keepdims=True)
    y = x * lax.rsqrt(ms + NORM_EPS) * g
    return y * (1.0 + sc) + sh


def _inproj_kernel(x_ref, g_ref, mod_ref, w_ref, o_ref, h_ref, *, sc_row, sh_row):
    @pl.when(pl.program_id(2) == 0)
    def _():
        m = mod_ref[0]
        h = _modnorm(x_ref[0], g_ref[...], m[sc_row:sc_row + 1], m[sh_row:sh_row + 1])
        h_ref[...] = h.astype(BF16)

    o_ref[0] = jnp.dot(h_ref[...], w_ref[...], preferred_element_type=F32).astype(o_ref.dtype)


def _inproj(x, g, mod, w_bf16, out_dtype, tm, tn):
    bsz, s, d = x.shape
    n = w_bf16.shape[1]
    return pl.pallas_call(
        functools.partial(_inproj_kernel, sc_row=1, sh_row=0), name="inproj",
        grid=(bsz, s // tm, n // tn),
        in_specs=[pl.BlockSpec((1, tm, d), lambda b, i, j: (b, i, 0)),
                  pl.BlockSpec((1, d), lambda b, i, j: (0, 0)),
                  pl.BlockSpec((1, 6, d), lambda b, i, j: (b, 0, 0)),
                  pl.BlockSpec((d, tn), lambda b, i, j: (0, j))],
        out_specs=pl.BlockSpec((1, tm, tn), lambda b, i, j: (b, i, j)),
        out_shape=jax.ShapeDtypeStruct((bsz, s, n), out_dtype),
        scratch_shapes=[pltpu.VMEM((tm, d), BF16)],
        compiler_params=_cparams(("arbitrary", "arbitrary", "arbitrary")),
    )(x, g.reshape(1, d), mod, w_bf16)


def _ssd_kernel(z_ref, xs_ref, bm_ref, cm_ref, dt_ref, cw_ref, cb_ref, dtb_ref, alog_ref,
                dsk_ref, ng_ref, e_ref, o_ref, ubuf_ref, state_ref, y_ref):
    L = SSM_CHUNK
    G = SSM_N_GROUPS
    N = SSM_D_STATE
    P = SSM_HEAD_DIM
    d_inner = xs_ref.shape[2]
    H = d_inner // P
    R = H // G
    GW = R * P
    c_idx = pl.program_id(1)

    @pl.when(c_idx == 0)
    def _():
        ubuf_ref[0:8, :] = jnp.zeros((8, ubuf_ref.shape[1]), F32)
        state_ref[...] = jnp.zeros(state_ref.shape, F32)

    ubuf_ref[8:8 + L, 0:d_inner] = xs_ref[0].astype(F32)
    ubuf_ref[8:8 + L, d_inner:d_inner + G * N] = bm_ref[0].astype(F32)
    ubuf_ref[8:8 + L, d_inner + G * N:] = cm_ref[0].astype(F32)
    conv = cb_ref[...]
    for j in range(SSM_CONV):
        conv = conv + cw_ref[j:j + 1, :] * ubuf_ref[5 + j:5 + j + L, :]
    ubuf_ref[0:8, :] = ubuf_ref[L:L + 8, :]
    conv = _silu(conv)
    xs = conv[:, :d_inner]
    bmat = conv[:, d_inner:d_inner + G * N].astype(BF16)
    cmat = conv[:, d_inner + G * N:].astype(BF16)

    dtx = dt_ref[0][:, :H] + dtb_ref[...]
    dt = jnp.maximum(dtx, 0.0) + jnp.log(1.0 + jnp.exp(-jnp.abs(dtx)))
    a_neg = -jnp.exp(alog_ref[...])
    d_a = dt * a_neg

    row = lax.broadcasted_iota(I32, (L, L), 0)
    col = lax.broadcasted_iota(I32, (L, L), 1)
    causal = col <= row
    lmat = jnp.where(causal, 1.0, 0.0).astype(BF16)
    acum = None
    for p in _split3(d_a):
        t = jnp.dot(lmat, p, preferred_element_type=F32)
        acum = t if acum is None else acum + t
    acum_t = jnp.transpose(acum)

    e_mat = e_ref[...]
    dt_e = _dot_x3(dt, e_mat)
    acum_e = _dot_x3(acum, e_mat)
    exp_a_e = jnp.exp(acum_e)
    to_end_e = jnp.exp(acum_e[L - 1:L, :] - acum_e)

    xdt = xs * dt_e
    xdt_b = xdt.astype(BF16)
    xend_b = (xdt * to_end_e).astype(BF16)

    for g in range(G):
        b_g = bmat[:, g * N:(g + 1) * N]
        c_g = cmat[:, g * N:(g + 1) * N]
        cb = lax.dot_general(c_g, b_g, (((1,), (1,)), ((), ())), preferred_element_type=F32)
        st = state_ref[g]
        y_off = jnp.dot(c_g, st.astype(BF16), preferred_element_type=F32)
        y_ref[:, g * GW:(g + 1) * GW] = y_off * exp_a_e[:, g * GW:(g + 1) * GW]
        upd = lax.dot_general(b_g, xend_b[:, g * GW:(g + 1) * GW], (((0,), (0,)), ((), ())),
                              preferred_element_type=F32)
        state_ref[g] = st * exp_a_e[L - 1:L, g * GW:(g + 1) * GW] + upd
        for r in range(R):
            h = g * R + r
            seg = acum[:, h:h + 1] - acum_t[h:h + 1, :]
            decay = jnp.where(causal, jnp.exp(jnp.minimum(seg, 0.0)), 0.0)
            m = (cb * decay).astype(BF16)
            y_h = jnp.dot(m, xdt_b[:, h * P:(h + 1) * P], preferred_element_type=F32)
            y_ref[:, h * P:(h + 1) * P] += y_h

    y = y_ref[...] + xs * dsk_ref[...]
    yg = y * _silu(z_ref[0].astype(F32))
    for g in range(G):
        blk = yg[:, g * GW:(g + 1) * GW]
        ms = jnp.mean(blk * blk, axis=-1, keepdims=True)
        o_ref[0, :, g * GW:(g + 1) * GW] = (
            blk * lax.rsqrt(ms + NORM_EPS) * ng_ref[:, g * GW:(g + 1) * GW]).astype(o_ref.dtype)


def _ssd(proj, small, conv_w, conv_b, dt_bias, a_log, d_skip, norm_g, d_inner):
    bsz, s, _ = proj.shape
    L = SSM_CHUNK
    G, N, P = SSM_N_GROUPS, SSM_D_STATE, SSM_HEAD_DIM
    H = d_inner // P
    cdim = d_inner + 2 * G * N
    gw = d_inner // G
    zb = d_inner // d_inner
    bc0 = 2 * d_inner // (G * N)
    e_mat = (jnp.arange(d_inner, dtype=I32)[None, :] // P == jnp.arange(H, dtype=I32)[:, None]).astype(BF16)
    return pl.pallas_call(
        _ssd_kernel, name="ssd",
        grid=(bsz, s // L),
        in_specs=[pl.BlockSpec((1, L, d_inner), lambda b, c: (b, c, 0)),
                  pl.BlockSpec((1, L, d_inner), lambda b, c: (b, c, zb)),
                  pl.BlockSpec((1, L, G * N), lambda b, c: (b, c, bc0)),
                  pl.BlockSpec((1, L, G * N), lambda b, c: (b, c, bc0 + 1)),
                  pl.BlockSpec((1, L, 128), lambda b, c: (b, c, 2)),
                  pl.BlockSpec((SSM_CONV, cdim), lambda b, c: (0, 0)),
                  pl.BlockSpec((1, cdim), lambda b, c: (0, 0)),
                  pl.BlockSpec((1, H), lambda b, c: (0, 0)),
                  pl.BlockSpec((1, H), lambda b, c: (0, 0)),
                  pl.BlockSpec((1, d_inner), lambda b, c: (0, 0)),
                  pl.BlockSpec((1, d_inner), lambda b, c: (0, 0)),
                  pl.BlockSpec((H, d_inner), lambda b, c: (0, 0))],
        out_specs=pl.BlockSpec((1, L, d_inner), lambda b, c: (b, c, 0)),
        out_shape=jax.ShapeDtypeStruct((bsz, s, d_inner), BF16),
        scratch_shapes=[pltpu.VMEM((L + 8, cdim), F32),
                        pltpu.VMEM((G, N, gw), F32),
                        pltpu.VMEM((L, d_inner), F32)],
        compiler_params=_cparams(("arbitrary", "arbitrary")),
    )(proj, proj, proj, proj, small, conv_w, conv_b.reshape(1, cdim), dt_bias.reshape(1, H),
      a_log.reshape(1, H), jnp.repeat(d_skip, P).reshape(1, d_inner), norm_g.reshape(1, d_inner), e_mat)


def _pack_w_in(w_in):
    d = w_in.shape[0]
    d_inner = 2 * d
    gn = SSM_N_GROUPS * SSM_D_STATE
    n_h = d_inner // SSM_HEAD_DIM
    sizes = (d_inner, d_inner + 2 * gn, n_h, ATT_N_HEADS * ATT_HEAD_DIM, ATT_N_KV_HEADS * ATT_HEAD_DIM,
             ATT_N_KV_HEADS * ATT_HEAD_DIM, IDX_N_HEADS * IDX_HEAD_DIM, IDX_HEAD_DIM, IDX_N_HEADS, d, d)
    pts = [int(p) for p in np.cumsum(sizes)[:-1]]
    z, xbc, dt, q, k, v, qi, ki, wi, g_ssm, g_att = jnp.split(w_in, pts, axis=1)
    big = jnp.concatenate([z, xbc, q, g_ssm, g_att, k, v, qi], axis=1).astype(BF16)
    zeros = lambda n: jnp.zeros((d, n), w_in.dtype)
    small = jnp.concatenate([ki, zeros(128 - IDX_HEAD_DIM), wi, zeros(128 - IDX_N_HEADS), dt, zeros(128 - n_h)],
                            axis=1).astype(BF16)
    return big, small


def _rope_kernel(q_ref, k_ref, qi_ref, sm_ref, wi_ref, ca_ref, sa_ref, ci_ref, si_ref,
                 qo_ref, ko_ref, qio_ref, kio_ref, wo_ref):
    ca = ca_ref[0]
    sa = sa_ref[0]
    ci = ci_ref[0]
    si = si_ref[0]
    att_scale = ATT_HEAD_DIM ** -0.5
    q = q_ref[0].astype(F32)
    for h in range(ATT_N_HEADS):
        x = q[:, h * 128:(h + 1) * 128]
        qo_ref[0, h] = ((x * ca + pltpu.roll(x, 64, 1) * sa) * att_scale).astype(qo_ref.dtype)
    k = k_ref[0].astype(F32)
    for h in range(ATT_N_KV_HEADS):
        x = k[:, h * 128:(h + 1) * 128]
        ko_ref[0, :, h * 128:(h + 1) * 128] = (x * ca + pltpu.roll(x, 64, 1) * sa).astype(ko_ref.dtype)
    lane = lax.broadcasted_iota(I32, ci.shape, 1)
    first_half = (lane % IDX_HEAD_DIM) < (IDX_HEAD_DIM // 2)

    def rot_i(x):
        sw = jnp.where(first_half, pltpu.roll(x, 128 - IDX_HEAD_DIM // 2, 1), pltpu.roll(x, IDX_HEAD_DIM // 2, 1))
        return x * ci + sw * si

    qi = qi_ref[0].astype(F32)
    for j in range(IDX_N_HEADS * IDX_HEAD_DIM // 128):
        qio_ref[0, :, j * 128:(j + 1) * 128] = rot_i(qi[:, j * 128:(j + 1) * 128]).astype(qio_ref.dtype)
    kio_ref[0] = rot_i(sm_ref[0]).astype(kio_ref.dtype)
    wo_ref[0] = wi_ref[0] * (IDX_HEAD_DIM ** -0.5 * IDX_N_HEADS ** -0.5)


def _rope(proj, small, positions, d_model, tm):
    bsz, s, _ = proj.shape
    d_inner = 2 * d_model
    nq = ATT_N_HEADS * ATT_HEAD_DIM
    nkv = ATT_N_KV_HEADS * ATT_HEAD_DIM
    nqi = IDX_N_HEADS * IDX_HEAD_DIM
    q_off = 2 * d_inner + 2 * SSM_N_GROUPS * SSM_D_STATE
    k_off = q_off + nq + 2 * d_model
    qi_off = k_off + 2 * nkv

    def tables(dim):
        inv = ROPE_THETA ** (-jnp.arange(0, dim, 2, dtype=F32) / dim)
        ang = positions.astype(F32)[..., None] * inv
        return jnp.cos(ang), jnp.sin(ang)

    cos_a, sin_a = tables(ATT_HEAD_DIM)
    cos_i, sin_i = tables(IDX_HEAD_DIM)
    ca = jnp.concatenate([cos_a, cos_a], -1)
    sa = jnp.concatenate([-sin_a, sin_a], -1)
    ci = jnp.concatenate([cos_i] * 4, -1)
    si = jnp.concatenate([-sin_i, sin_i, -sin_i, sin_i], -1)
    tab = pl.BlockSpec((1, tm, 128), lambda b, i: (b, i, 0))
    return pl.pallas_call(
        _rope_kernel, name="rope",
        grid=(bsz, s // tm),
        in_specs=[pl.BlockSpec((1, tm, nq), lambda b, i: (b, i, q_off // nq)),
                  pl.BlockSpec((1, tm, nkv), lambda b, i: (b, i, k_off // nkv)),
                  pl.BlockSpec((1, tm, nqi), lambda b, i: (b, i, qi_off // nqi)),
                  pl.BlockSpec((1, tm, 128), lambda b, i: (b, i, 0)),
                  pl.BlockSpec((1, tm, 128), lambda b, i: (b, i, 1)),
                  tab, tab, tab, tab],
        out_specs=[pl.BlockSpec((1, ATT_N_HEADS, tm, 128), lambda b, i: (b, 0, i, 0)),
                   pl.BlockSpec((1, tm, nkv), lambda b, i: (b, i, 0)),
                   pl.BlockSpec((1, tm, nqi), lambda b, i: (b, i, 0)),
                   tab, tab],
        out_shape=[jax.ShapeDtypeStruct((bsz, ATT_N_HEADS, s, 128), BF16),
                   jax.ShapeDtypeStruct((bsz, s, nkv), BF16),
                   jax.ShapeDtypeStruct((bsz, s, nqi), BF16),
                   jax.ShapeDtypeStruct((bsz, s, 128), BF16),
                   jax.ShapeDtypeStruct((bsz, s, 128), F32)],
        compiler_params=_cparams(("arbitrary", "arbitrary")),
    )(proj, proj, proj, small, small, ca, sa, ci, si)


DSA_TQ = 256
DSA_TK = 512


def _score_keys(qi, w, k_t):
    kk = k_t[:, :IDX_HEAD_DIM]
    sc = None
    for h in range(IDX_N_HEADS):
        l = lax.dot_general(qi[:, h * IDX_HEAD_DIM:(h + 1) * IDX_HEAD_DIM], kk, (((1,), (1,)), ((), ())),
                            preferred_element_type=F32)
        t = jnp.maximum(l, 0.0) * w[:, h:h + 1]
        sc = t if sc is None else sc + t
    bits = lax.bitcast_convert_type(sc, I32)
    bits = jnp.where(bits == INT_MIN, 0, bits)
    return bits ^ ((bits >> 31) & 0x7FFFFFFF)


def _thresh_kernel(qi_ref, w_ref, ki_ref, o_ref, keys_ref, *, topk, seq):
    tq, tk = DSA_TQ, DSA_TK
    i = pl.program_id(1)
    nkt = (i * tq + tq - 1) // tk + 1
    qi = qi_ref[0]
    w = w_ref[0]
    qpos = i * tq + lax.broadcasted_iota(I32, (tq, tk), 0)
    lane_pos = lax.broadcasted_iota(I32, (tq, tk), 1)

    def fill(t, carry):
        key = _score_keys(qi, w, ki_ref[0, pl.ds(pl.multiple_of(t * tk, tk), tk), :])
        keys_ref[t] = jnp.where(t * tk + lane_pos <= qpos, key, INT_MIN)
        return carry

    lax.fori_loop(0, nkt, fill, 0)

    def count(pred):
        def body(t, acc):
            c = jnp.where(pred(keys_ref[t], t), 1.0, 0.0)
            s = c[:, 0:128]
            for u in range(1, tk // 128):
                s = s + c[:, u * 128:(u + 1) * 128]
            return acc + s
        acc = lax.fori_loop(0, nkt, body, jnp.zeros((tq, 128), F32))
        return jnp.sum(acc, axis=-1, keepdims=True)

    kf = float(topk)
    c0 = count(lambda kt, t: kt >= 0)
    thr = jnp.where(c0 >= kf, 0, INT_MIN).astype(I32)

    def bit_body(bi, thr):
        cand = thr | jnp.left_shift(jnp.int32(1), 30 - bi)
        cand_b = jnp.broadcast_to(cand, (tq, tk))
        c = count(lambda kt, t: kt >= cand_b)
        return jnp.where(c >= kf, cand, thr)

    thr = lax.fori_loop(0, 31, bit_body, thr)
    thr_b = jnp.broadcast_to(thr, (tq, tk))
    need = kf - count(lambda kt, t: kt > thr_b)
    n_eq = count(lambda kt, t: kt == thr_b)
    lane = lax.broadcasted_iota(I32, (tq, 128), 1)

    def emit(cut):
        o_ref[0] = jnp.where(lane == 0, thr, jnp.where(lane == 1, cut, 0))

    emit(jnp.full((tq, 1), seq, I32))
    partial = jnp.logical_and(n_eq > need, thr > INT_MIN)

    @pl.when(jnp.max(jnp.where(partial, 1.0, 0.0)) > 0.0)
    def _():
        def idx_body(bi, cut):
            cand = cut | jnp.left_shift(jnp.int32(1), (seq.bit_length() - 1) - bi)
            cand_b = jnp.broadcast_to(cand, (tq, tk))
            c = count(lambda kt, t: jnp.logical_and(kt == thr_b, t * tk + lane_pos < cand_b))
            return jnp.where(c < need, cand, cut)
        emit(lax.fori_loop(0, seq.bit_length(), idx_body, jnp.zeros((tq, 1), I32)))


def _thresh(qi_rot, w_s, ki_rot, topk):
    bsz, s, nqi = qi_rot.shape
    tq, tk = DSA_TQ, DSA_TK
    return pl.pallas_call(
        functools.partial(_thresh_kernel, topk=topk, seq=s), name="dsa_thresh",
        grid=(bsz, s // tq),
        in_specs=[pl.BlockSpec((1, tq, nqi), lambda b, i: (b, i, 0)),
                  pl.BlockSpec((1, tq, 128), lambda b, i: (b, i, 0)),
                  pl.BlockSpec((1, s, 128), lambda b, i: (b, 0, 0))],
        out_specs=pl.BlockSpec((1, tq, 128), lambda b, i: (b, i, 0)),
        out_shape=jax.ShapeDtypeStruct((bsz, s, 128), I32),
        scratch_shapes=[pltpu.VMEM((s // tk, tq, tk), I32)],
        compiler_params=_cparams(("arbitrary", "arbitrary")),
    )(qi_rot, w_s, ki_rot)


def _attn_kernel(qb_ref, kt_ref, q_ref, k_ref, v_ref, qi_ref, w_ref, ki_ref, thr_ref, o_ref,
                 m_ref, l_ref, acc_ref):
    tq, tk = DSA_TQ, DSA_TK
    rep = ATT_N_HEADS // ATT_N_KV_HEADS
    step = pl.program_id(1)
    qb = qb_ref[step]
    kt = kt_ref[step]

    @pl.when(kt == 0)
    def _():
        m_ref[...] = jnp.full(m_ref.shape, NEG_BIG, F32)
        l_ref[...] = jnp.zeros(l_ref.shape, F32)
        acc_ref[...] = jnp.zeros(acc_ref.shape, F32)

    key = _score_keys(qi_ref[0], w_ref[0], ki_ref[0])
    thr_blk = thr_ref[0]
    thr = thr_blk[:, 0:1]
    cut = thr_blk[:, 1:2]
    qpos = qb * tq + lax.broadcasted_iota(I32, (tq, tk), 0)
    kpos = kt * tk + lax.broadcasted_iota(I32, (tq, tk), 1)
    sel = jnp.logical_or(key > thr, jnp.logical_and(key == thr, kpos <= cut))
    bias = jnp.where(jnp.logical_and(kpos <= qpos, sel), 0.0, NEG_BIG)

    for g in range(ATT_N_KV_HEADS):
        qg = q_ref[0, g * rep:(g + 1) * rep].reshape(rep * tq, ATT_HEAD_DIM)
        kg = k_ref[0, :, g * 128:(g + 1) * 128]
        vg = v_ref[0, :, g * 128:(g + 1) * 128]
        s = lax.dot_general(qg, kg, (((1,), (1,)), ((), ())), preferred_element_type=F32)
        s = (s.reshape(rep, tq, tk) + bias[None]).reshape(rep * tq, tk)
        m_old = m_ref[g]
        m_new = jnp.maximum(m_old, jnp.max(s, axis=-1, keepdims=True))
        alpha = jnp.exp(m_old - m_new)
        p = jnp.exp(s - m_new[:, 0:1])
        l_ref[g] = alpha * l_ref[g] + jnp.sum(p, axis=-1, keepdims=True)
        acc_ref[g] = alpha * acc_ref[g] + jnp.dot(p.astype(vg.dtype), vg, preferred_element_type=F32)
        m_ref[g] = m_new

    @pl.when(kt == (qb * tq + tq - 1) // tk)
    def _():
        for g in range(ATT_N_KV_HEADS):
            o = acc_ref[g] / l_ref[g]
            for r in range(rep):
                h = g * rep + r
                o_ref[0, :, h * 128:(h + 1) * 128] = o[r * tq:(r + 1) * tq].astype(o_ref.dtype)


def _attn(q_rot, k_rot, proj, qi_rot, w_s, ki_rot, thr, d_model):
    bsz, n_h, s, _ = q_rot.shape
    tq, tk = DSA_TQ, DSA_TK
    nkv = ATT_N_KV_HEADS * ATT_HEAD_DIM
    nqi = qi_rot.shape[2]
    d_inner = 2 * d_model
    v_off = 2 * d_inner + 2 * SSM_N_GROUPS * SSM_D_STATE + n_h * ATT_HEAD_DIM + 2 * d_model + nkv
    pairs = [(qb, kt) for qb in range(s // tq) for kt in range((qb * tq + tq - 1) // tk + 1)]
    qb_arr = jnp.asarray([p[0] for p in pairs], I32)
    kt_arr = jnp.asarray([p[1] for p in pairs], I32)
    rep = n_h // ATT_N_KV_HEADS
    grid_spec = pltpu.PrefetchScalarGridSpec(
        num_scalar_prefetch=2,
        grid=(bsz, len(pairs)),
        in_specs=[pl.BlockSpec((1, n_h, tq, 128), lambda b, t, qb, kt: (b, 0, qb[t], 0)),
                  pl.BlockSpec((1, tk, nkv), lambda b, t, qb, kt: (b, kt[t], 0)),
                  pl.BlockSpec((1, tk, nkv), lambda b, t, qb, kt: (b, kt[t], v_off // nkv)),
                  pl.BlockSpec((1, tq, nqi), lambda b, t, qb, kt: (b, qb[t], 0)),
                  pl.BlockSpec((1, tq, 128), lambda b, t, qb, kt: (b, qb[t], 0)),
                  pl.BlockSpec((1, tk, 128), lambda b, t, qb, kt: (b, kt[t], 0)),
                  pl.BlockSpec((1, tq, 128), lambda b, t, qb, kt: (b, qb[t], 0))],
        out_specs=pl.BlockSpec((1, tq, n_h * 128), lambda b, t, qb, kt: (b, qb[t], 0)),
        scratch_shapes=[pltpu.VMEM((ATT_N_KV_HEADS, rep * tq, 128), F32),
                        pltpu.VMEM((ATT_N_KV_HEADS, rep * tq, 128), F32),
                        pltpu.VMEM((ATT_N_KV_HEADS, rep * tq, 128), F32)])
    return pl.pallas_call(
        _attn_kernel, name="dsa_attn",
        grid_spec=grid_spec,
        out_shape=jax.ShapeDtypeStruct((bsz, s, n_h * 128), BF16),
        compiler_params=_cparams(("arbitrary", "arbitrary")),
    )(qb_arr, kt_arr, q_rot, k_rot, proj, qi_rot, w_s, ki_rot, thr)


def _merge_kernel(y_ref, o_ref, gs_ref, ga_ref, ws_ref, wa_ref, out_ref):
    a = jnp.dot(y_ref[0], ws_ref[...], preferred_element_type=F32)
    b = jnp.dot(o_ref[0], wa_ref[...], preferred_element_type=F32)
    out_ref[0] = (_sigmoid(gs_ref[0].astype(F32)) * a + _sigmoid(ga_ref[0].astype(F32)) * b).astype(out_ref.dtype)


def _merge(y_ssm, o_att, proj, w_ssm, w_att, tm, tn):
    bsz, s, d_inner = y_ssm.shape
    d_att = o_att.shape[2]
    d = w_ssm.shape[1]
    gs_off = 2 * d_inner + 2 * SSM_N_GROUPS * SSM_D_STATE + d_att
    ga_off = gs_off + d
    return pl.pallas_call(
        _merge_kernel, name="merge",
        grid=(bsz, s // tm, d // tn),
        in_specs=[pl.BlockSpec((1, tm, d_inner), lambda b, i, j: (b, i, 0)),
                  pl.BlockSpec((1, tm, d_att), lambda b, i, j: (b, i, 0)),
                  pl.BlockSpec((1, tm, tn), lambda b, i, j: (b, i, gs_off // tn + j)),
                  pl.BlockSpec((1, tm, tn), lambda b, i, j: (b, i, ga_off // tn + j)),
                  pl.BlockSpec((d_inner, tn), lambda b, i, j: (0, j)),
                  pl.BlockSpec((d_att, tn), lambda b, i, j: (0, j))],
        out_specs=pl.BlockSpec((1, tm, tn), lambda b, i, j: (b, i, j)),
        out_shape=jax.ShapeDtypeStruct((bsz, s, d), BF16),
        compiler_params=_cparams(("arbitrary", "arbitrary", "arbitrary")),
    )(y_ssm, o_att, proj, proj, w_ssm, w_att)


def _post_kernel(m_ref, wo_ref, x_ref, mod_ref, g_ref, rw_ref, rb_ref, x1_ref, h2_ref, gate_ref, idx_ref):
    mod = mod_ref[0]
    out = jnp.dot(m_ref[0], wo_ref[...], preferred_element_type=F32)
    x1 = x_ref[0] + mod[2:3] * out
    x1_ref[0] = x1
    h2 = _modnorm(x1, g_ref[...], mod[4:5], mod[3:4])
    h2_ref[0] = h2
    logit = _dot_hi(h2, rw_ref[...]) + rb_ref[...]
    lane = lax.broadcasted_iota(I32, logit.shape, 1)
    lane_f = lane.astype(F32)
    vals, idxs = [], []
    for _ in range(TOP_K):
        m = jnp.max(logit, axis=-1, keepdims=True)
        idx = jnp.min(jnp.where(logit == m, lane_f, 128.0), axis=-1, keepdims=True)
        vals.append(m)
        idxs.append(idx)
        logit = jnp.where(lane_f == idx, -3e38, logit)
    es = [jnp.exp(v - vals[0]) for v in vals]
    denom = es[0] + es[1] + es[2] + es[3]
    gate = jnp.zeros(logit.shape, F32)
    sel = jnp.zeros(logit.shape, F32)
    for k in range(TOP_K):
        gate = jnp.where(lane == k, es[k] / denom, gate)
        sel = jnp.where(lane == k, idxs[k], sel)
    gate_ref[0] = gate
    idx_ref[0] = sel.astype(I32)


def _post(merged, w_o, x, mod, g, router_w, router_b, tm):
    bsz, s, d = x.shape
    n_e = router_w.shape[1]
    rw = jnp.zeros((d, 128), F32).at[:, :n_e].set(router_w.astype(F32))
    rb = jnp.full((1, 128), NEG_BIG, F32).at[0, :n_e].set(router_b.astype(F32))
    row = lambda w: pl.BlockSpec((1, tm, w), lambda b, i: (b, i, 0))
    return pl.pallas_call(
        _post_kernel, name="post_router",
        grid=(bsz, s // tm),
        in_specs=[row(d),
                  pl.BlockSpec((d, d), lambda b, i: (0, 0)),
                  row(d),
                  pl.BlockSpec((1, 6, d), lambda b, i: (b, 0, 0)),
                  pl.BlockSpec((1, d), lambda b, i: (0, 0)),
                  pl.BlockSpec((d, 128), lambda b, i: (0, 0)),
                  pl.BlockSpec((1, 128), lambda b, i: (0, 0))],
        out_specs=[row(d), row(d), row(128), row(128)],
        out_shape=[jax.ShapeDtypeStruct((bsz, s, d), F32),
                   jax.ShapeDtypeStruct((bsz, s, d), F32),
                   jax.ShapeDtypeStruct((bsz, s, 128), F32),
                   jax.ShapeDtypeStruct((bsz, s, 128), I32)],
        compiler_params=_cparams(("arbitrary", "arbitrary")),
    )(merged, w_o, x, mod, g.reshape(1, d), rw, rb)


MOE_TM = 512
MOE_TN = 512


def _moe_kernel(be_ref, nu_ref, rt_ref, rd_ref, h_hbm, wg_ref, wu_ref, bg_ref, bu_ref, wd_ref, bd_ref,
                y_hbm, xf_ref, xb_ref, acc_ref, yo_ref, sem):
    tm = MOE_TM
    blk = pl.program_id(0)
    j = pl.program_id(1)
    n_j = pl.num_programs(1)

    def row_in(r, tok):
        return pltpu.make_async_copy(h_hbm.at[pl.ds(tok, 1)], xf_ref.at[pl.ds(r, 1)], sem.at[0])

    def row_out(r, dst):
        return pltpu.make_async_copy(yo_ref.at[pl.ds(r, 1)], y_hbm.at[pl.ds(dst, 1)], sem.at[1])

    @pl.when(blk < nu_ref[0])
    def _():
        @pl.when(j == 0)
        def _():
            def start(r, c):
                row_in(r, rt_ref[0, 0, r]).start()
                return c
            lax.fori_loop(0, tm, start, 0)

            def wait(r, c):
                row_in(r, 0).wait()
                return c
            lax.fori_loop(0, tm, wait, 0)
            xb_ref[...] = xf_ref[...].astype(BF16)
            acc_ref[...] = jnp.zeros(acc_ref.shape, F32)

        x = xb_ref[...]
        gate = jnp.dot(x, wg_ref[0], preferred_element_type=F32) + bg_ref[0]
        up = jnp.dot(x, wu_ref[0], preferred_element_type=F32) + bu_ref[0]
        gate = jnp.minimum(gate, SWIGLU_LIMIT)
        up = jnp.clip(up, -SWIGLU_LIMIT, SWIGLU_LIMIT)
        act = (up + 1.0) * (gate * _sigmoid(SWIGLU_ALPHA * gate))
        acc_ref[...] += jnp.dot(act.astype(BF16), wd_ref[0], preferred_element_type=F32)

        @pl.when(j == n_j - 1)
        def _():
            yo_ref[...] = acc_ref[...] + bd_ref[0]

            def start(r, c):
                dst = rd_ref[0, 0, r]

                @pl.when(dst >= 0)
                def _():
                    row_out(r, dst).start()
                return c
            lax.fori_loop(0, tm, start, 0)

            def wait(r, c):
                @pl.when(rd_ref[0, 0, r] >= 0)
                def _():
                    row_out(r, 0).wait()
                return c
            lax.fori_loop(0, tm, wait, 0)


def _moe(h2, gate_idx, w_gu, b_gu, w_dn, b_dn):
    n_tok, d = h2.shape
    n_e, _, d2 = w_gu.shape
    d_e = d2 // 2
    tm, tn = MOE_TM, MOE_TN
    n_j = d_e // tn
    n_asg = n_tok * TOP_K
    e_flat = gate_idx[:, :TOP_K].reshape(-1)
    onehot = (e_flat[:, None] == jnp.arange(n_e, dtype=I32)[None, :]).astype(I32)
    csum = jnp.cumsum(onehot, axis=0)
    rank = jnp.take_along_axis(csum, e_flat[:, None], axis=1)[:, 0] - 1
    counts = csum[-1]
    padded = (counts + tm - 1) // tm * tm
    ends = jnp.cumsum(padded)
    dest = (ends - padded)[e_flat] + rank
    n_blocks = -(-(n_asg + n_e * (tm - 1)) // tm)
    asg = jnp.arange(n_asg, dtype=I32)
    row_tok = jnp.zeros((n_blocks * tm,), I32).at[dest].set(asg // TOP_K).reshape(n_blocks, 1, tm)
    row_dst = jnp.full((n_blocks * tm,), -1, I32).at[dest].set(asg).reshape(n_blocks, 1, tm)
    block_expert = jnp.minimum(
        jnp.searchsorted(ends, jnp.arange(n_blocks, dtype=I32) * tm, side='right'), n_e - 1).astype(I32)
    n_used = (ends[-1:] // tm).astype(I32)

    def jj(blk, j, nu):
        return jnp.where(blk < nu[0], j, n_j - 1)

    smem_rows = pl.BlockSpec((1, 1, tm), lambda blk, j, be, nu: (blk, 0, 0), memory_space=pltpu.SMEM)
    grid_spec = pltpu.PrefetchScalarGridSpec(
        num_scalar_prefetch=2,
        grid=(n_blocks, n_j),
        in_specs=[smem_rows, smem_rows,
                  pl.BlockSpec(memory_space=pl.ANY),
                  pl.BlockSpec((1, d, tn), lambda blk, j, be, nu: (be[blk], 0, jj(blk, j, nu))),
                  pl.BlockSpec((1, d, tn), lambda blk, j, be, nu: (be[blk], 0, n_j + jj(blk, j, nu))),
                  pl.BlockSpec((1, 1, tn), lambda blk, j, be, nu: (be[blk], 0, jj(blk, j, nu))),
                  pl.BlockSpec((1, 1, tn), lambda blk, j, be, nu: (be[blk], 0, n_j + jj(blk, j, nu))),
                  pl.BlockSpec((1, tn, d), lambda blk, j, be, nu: (be[blk], jj(blk, j, nu), 0)),
                  pl.BlockSpec((1, 1, d), lambda blk, j, be, nu: (be[blk], 0, 0))],
        out_specs=pl.BlockSpec(memory_space=pl.ANY),
        scratch_shapes=[pltpu.VMEM((tm, d), F32),
                        pltpu.VMEM((tm, d), BF16),
                        pltpu.VMEM((tm, d), F32),
                        pltpu.VMEM((tm, d), F32),
                        pltpu.SemaphoreType.DMA((2,))])
    return pl.pallas_call(
        _moe_kernel, name="moe_experts",
        grid_spec=grid_spec,
        out_shape=jax.ShapeDtypeStruct((n_asg, d), F32),
        compiler_params=_cparams(("arbitrary", "arbitrary")),
    )(block_expert, n_used, row_tok, row_dst, h2, w_gu, w_gu, b_gu.reshape(n_e, 1, d2),
      b_gu.reshape(n_e, 1, d2), w_dn, b_dn.reshape(n_e, 1, d))


def _combine_kernel(x1_ref, y_ref, gate_ref, mod_ref, g_ref, o_ref, *, final):
    d = x1_ref.shape[2]
    gate = gate_ref[0]
    moe = gate[:, 0:1] * y_ref[0, :, 0:d]
    for k in range(1, TOP_K):
        moe = moe + gate[:, k:k + 1] * y_ref[0, :, k * d:(k + 1) * d]
    x2 = x1_ref[0] + mod_ref[0][5:6] * moe
    if final:
        ms = jnp.mean(x2 * x2, axis=-1, keepdims=True)
        x2 = x2 * lax.rsqrt(ms + NORM_EPS) * g_ref[...]
    o_ref[0] = x2


def _combine(x1, y, gates, mod, g, final, tm):
    bsz, s, d = x1.shape
    row = lambda w: pl.BlockSpec((1, tm, w), lambda b, i: (b, i, 0))
    return pl.pallas_call(
        functools.partial(_combine_kernel, final=final), name="combine",
        grid=(bsz, s // tm),
        in_specs=[row(d), row(TOP_K * d), row(128),
                  pl.BlockSpec((1, 6, d), lambda b, i: (b, 0, 0)),
                  pl.BlockSpec((1, d), lambda b, i: (0, 0))],
        out_specs=row(d),
        out_shape=jax.ShapeDtypeStruct((bsz, s, d), F32),
        compiler_params=_cparams(("arbitrary", "arbitrary")),
    )(x1, y.reshape(bsz, s, TOP_K * d), gates, mod, g.reshape(1, d))


def _pick(n, *cands):
    for c in cands:
        if n % c == 0:
            return c
    return n


def kernel(x, c, positions, ada_w, ada_b, norm_mix_g, w_in, conv_w, conv_b, dt_bias, a_log, d_skip, ssm_norm_g,
           w_out_ssm, w_out_att, w_o, norm_ffn_g, router_w, router_b, expert_w_gate_up, expert_b_gate_up,
           expert_w_down, expert_b_down, final_norm_g):
    bsz, s, d = x.shape
    depth = ada_w.shape[0]
    topk = min(IDX_TOPK_MAX, s // 4)
    for l in range(depth):
        mod = _adaln(c, ada_w[l], ada_b[l]).reshape(bsz, 6, d)
        w_big, w_small = _pack_w_in(w_in[l])
        tm = _pick(s, 1024, 512, 256)
        proj = _inproj(x, norm_mix_g[l], mod, w_big, BF16, tm, 1280)
        small = _inproj(x, norm_mix_g[l], mod, w_small, F32, tm, w_small.shape[1])
        y_ssm = _ssd(proj, small, conv_w[l], conv_b[l], dt_bias[l], a_log[l], d_skip[l], ssm_norm_g[l], 2 * d)
        q_rot, k_rot, qi_rot, ki_rot, w_s = _rope(proj, small, positions, d, _pick(s, 512, 256))
        thr = _thresh(qi_rot, w_s, ki_rot, topk)
        o_att = _attn(q_rot, k_rot, proj, qi_rot, w_s, ki_rot, thr, d)
        merged = _merge(y_ssm, o_att, proj, w_out_ssm[l].astype(BF16), w_out_att[l].astype(BF16),
                        _pick(s, 512, 256), 1024)
        x1, h2, gates, gate_idx = _post(merged, w_o[l].astype(BF16), x, mod, norm_ffn_g[l], router_w[l],
                                        router_b[l], 256)
        y = _moe(h2.reshape(bsz * s, d), gate_idx.reshape(bsz * s, 128), expert_w_gate_up[l].astype(BF16),
                 expert_b_gate_up[l], expert_w_down[l].astype(BF16), expert_b_down[l])
        x = _combine(x1, y, gates, mod, final_norm_g, l == depth - 1, 256)
    return x
```

```python
import functools

import jax
import jax.numpy as jnp
import numpy as np
from jax import lax
from jax.experimental import pallas as pl
from jax.experimental.pallas import tpu as pltpu

F32 = jnp.float32
BF16 = jnp.bfloat16
I32 = jnp.int32

SSM_HEAD_DIM = 64
SSM_N_GROUPS = 8
SSM_D_STATE = 128
SSM_CONV = 4
SSM_CHUNK = 128
ATT_N_HEADS = 16
ATT_N_KV_HEADS = 4
ATT_HEAD_DIM = 128
IDX_N_HEADS = 8
IDX_HEAD_DIM = 64
IDX_TOPK_MAX = 256
ROPE_THETA = 10000.0
N_EXPERTS = 32
TOP_K = 4
SWIGLU_ALPHA = 1.702
SWIGLU_LIMIT = 7.0
NORM_EPS = 1e-6

INT_MIN = -2147483648
NEG_BIG = -1e30
VMEM_LIMIT = 56 * 1024 * 1024


def _cparams(sem):
    return pltpu.CompilerParams(dimension_semantics=sem, vmem_limit_bytes=VMEM_LIMIT)


def _split3(a):
    hi = a.astype(BF16)
    r1 = a - hi.astype(F32)
    mid = r1.astype(BF16)
    lo = (r1 - mid.astype(F32)).astype(BF16)
    return hi, mid, lo


def _dot_x3(a, b_bf16, dims=None):
    out = None
    for p in _split3(a):
        if dims is None:
            t = jnp.dot(p, b_bf16, preferred_element_type=F32)
        else:
            t = lax.dot_general(p, b_bf16, dims, preferred_element_type=F32)
        out = t if out is None else out + t
    return out


def _dot_hi(a, b):
    a0, a1, a2 = _split3(a)
    b0, b1, b2 = _split3(b)
    d = functools.partial(jnp.dot, preferred_element_type=F32)
    return (d(a0, b0) + (d(a0, b1) + d(a1, b0))
            + (d(a0, b2) + d(a2, b0) + d(a1, b1)))


def _sigmoid(x):
    return 1.0 / (1.0 + jnp.exp(-x))


def _silu(x):
    return x * _sigmoid(x)


def _adaln_kernel(c_ref, w_ref, b_ref, o_ref):
    c = c_ref[...]
    o_ref[...] = _dot_hi(_silu(c), w_ref[...]) + b_ref[...]


def _adaln(c, ada_w, ada_b):
    bsz, d = c.shape
    n = ada_w.shape[1]
    rows = 8
    cp = jnp.zeros((rows, d), F32).at[:bsz].set(c)
    tn = 1024
    out = pl.pallas_call(
        _adaln_kernel, name="adaln",
        grid=(n // tn,),
        in_specs=[pl.BlockSpec((rows, d), lambda j: (0, 0)),
                  pl.BlockSpec((d, tn), lambda j: (0, j)),
                  pl.BlockSpec((1, tn), lambda j: (0, j))],
        out_specs=pl.BlockSpec((rows, tn), lambda j: (0, j)),
        out_shape=jax.ShapeDtypeStruct((rows, n), F32),
        compiler_params=_cparams(("arbitrary",)),
    )(cp, ada_w, ada_b.reshape(1, n))
    return out[:bsz]


def _modnorm(x, g, sc, sh):
    ms = jnp.mean(x * x, axis=-1, keepdims=True)
    y = x * lax.rsqrt(ms + NORM_EPS) * g
    return y * (1.0 + sc) + sh


def _inproj_kernel(x_ref, g_ref, mod_ref, w_ref, o_ref, h_ref, *, sc_row, sh_row):
    @pl.when(pl.program_id(2) == 0)
    def _():
        m = mod_ref[0]
        h = _modnorm(x_ref[0], g_ref[...], m[sc_row:sc_row + 1], m[sh_row:sh_row + 1])
        h_ref[...] = h.astype(BF16)

    o_ref[0] = jnp.dot(h_ref[...], w_ref[...], preferred_element_type=F32).astype(o_ref.dtype)


def _inproj(x, g, mod, w_bf16, out_dtype, tm, tn):
    bsz, s, d = x.shape
    n = w_bf16.shape[1]
    return pl.pallas_call(
        functools.partial(_inproj_kernel, sc_row=1, sh_row=0), name="inproj",
        grid=(bsz, s // tm, n // tn),
        in_specs=[pl.BlockSpec((1, tm, d), lambda b, i, j: (b, i, 0)),
                  pl.BlockSpec((1, d), lambda b, i, j: (0, 0)),
                  pl.BlockSpec((1, 6, d), lambda b, i, j: (b, 0, 0)),
                  pl.BlockSpec((d, tn), lambda b, i, j: (0, j))],
        out_specs=pl.BlockSpec((1, tm, tn), lambda b, i, j: (b, i, j)),
        out_shape=jax.ShapeDtypeStruct((bsz, s, n), out_dtype),
        scratch_shapes=[pltpu.VMEM((tm, d), BF16)],
        compiler_params=_cparams(("arbitrary", "arbitrary", "arbitrary")),
    )(x, g.reshape(1, d), mod, w_bf16)


def _ssd_kernel(z_ref, xs_ref, bm_ref, cm_ref, dt_ref, cw_ref, cb_ref, dtb_ref, alog_ref,
                dsk_ref, ng_ref, e_ref, o_ref, ubuf_ref, state_ref, y_ref):
    L = SSM_CHUNK
    G = SSM_N_GROUPS
    N = SSM_D_STATE
    P = SSM_HEAD_DIM
    d_inner = xs_ref.shape[2]
    H = d_inner // P
    R = H // G
    GW = R * P
    c_idx = pl.program_id(1)

    @pl.when(c_idx == 0)
    def _():
        ubuf_ref[0:8, :] = jnp.zeros((8, ubuf_ref.shape[1]), F32)
        state_ref[...] = jnp.zeros(state_ref.shape, F32)

    ubuf_ref[8:8 + L, 0:d_inner] = xs_ref[0].astype(F32)
    ubuf_ref[8:8 + L, d_inner:d_inner + G * N] = bm_ref[0].astype(F32)
    ubuf_ref[8:8 + L, d_inner + G * N:] = cm_ref[0].astype(F32)
    conv = cb_ref[...]
    for j in range(SSM_CONV):
        conv = conv + cw_ref[j:j + 1, :] * ubuf_ref[5 + j:5 + j + L, :]
    ubuf_ref[0:8, :] = ubuf_ref[L:L + 8, :]
    conv = _silu(conv)
    xs = conv[:, :d_inner]
    bmat = conv[:, d_inner:d_inner + G * N].astype(BF16)
    cmat = conv[:, d_inner + G * N:].astype(BF16)

    dtx = dt_ref[0][:, :H] + dtb_ref[...]
    dt = jnp.maximum(dtx, 0.0) + jnp.log(1.0 + jnp.exp(-jnp.abs(dtx)))
    a_neg = -jnp.exp(alog_ref[...])
    d_a = dt * a_neg

    row = lax.broadcasted_iota(I32, (L, L), 0)
    col = lax.broadcasted_iota(I32, (L, L), 1)
    causal = col <= row
    lmat = jnp.where(causal, 1.0, 0.0).astype(BF16)
    acum = None
    for p in _split3(d_a):
        t = jnp.dot(lmat, p, preferred_element_type=F32)
        acum = t if acum is None else acum + t
    acum_t = jnp.transpose(acum)

    e_mat = e_ref[...]
    dt_e = _dot_x3(dt, e_mat)
    acum_e = _dot_x3(acum, e_mat)
    exp_a_e = jnp.exp(acum_e)
    to_end_e = jnp.exp(acum_e[L - 1:L, :] - acum_e)

    xdt = xs * dt_e
    xdt_b = xdt.astype(BF16)
    xend_b = (xdt * to_end_e).astype(BF16)

    for g in range(G):
        b_g = bmat[:, g * N:(g + 1) * N]
        c_g = cmat[:, g * N:(g + 1) * N]
        cb = lax.dot_general(c_g, b_g, (((1,), (1,)), ((), ())), preferred_element_type=F32)
        st = state_ref[g]
        y_off = jnp.dot(c_g, st.astype(BF16), preferred_element_type=F32)
        y_ref[:, g * GW:(g + 1) * GW] = y_off * exp_a_e[:, g * GW:(g + 1) * GW]
        upd = lax.dot_general(b_g, xend_b[:, g * GW:(g + 1) * GW], (((0,), (0,)), ((), ())),
                              preferred_element_type=F32)
        state_ref[g] = st * exp_a_e[L - 1:L, g * GW:(g + 1) * GW] + upd
        for r in range(R):
            h = g * R + r
            seg = acum[:, h:h + 1] - acum_t[h:h + 1, :]
            decay = jnp.where(causal, jnp.exp(jnp.minimum(seg, 0.0)), 0.0)
            m = (cb * decay).astype(BF16)
            y_h = jnp.dot(m, xdt_b[:, h * P:(h + 1) * P], preferred_element_type=F32)
            y_ref[:, h * P:(h + 1) * P] += y_h

    y = y_ref[...] + xs * dsk_ref[...]
    yg = y * _silu(z_ref[0].astype(F32))
    for g in range(G):
        blk = yg[:, g * GW:(g + 1) * GW]
        ms = jnp.mean(blk * blk, axis=-1, keepdims=True)
        o_ref[0, :, g * GW:(g + 1) * GW] = (
            blk * lax.rsqrt(ms + NORM_EPS) * ng_ref[:, g * GW:(g + 1) * GW]).astype(o_ref.dtype)


def _ssd(proj, small, conv_w, conv_b, dt_bias, a_log, d_skip, norm_g, d_inner):
    bsz, s, _ = proj.shape
    L = SSM_CHUNK
    G, N, P = SSM_N_GROUPS, SSM_D_STATE, SSM_HEAD_DIM
    H = d_inner // P
    cdim = d_inner + 2 * G * N
    gw = d_inner // G
    zb = d_inner // d_inner
    bc0 = 2 * d_inner // (G * N)
    e_mat = (jnp.arange(d_inner, dtype=I32)[None, :] // P == jnp.arange(H, dtype=I32)[:, None]).astype(BF16)
    return pl.pallas_call(
        _ssd_kernel, name="ssd",
        grid=(bsz, s // L),
        in_specs=[pl.BlockSpec((1, L, d_inner), lambda b, c: (b, c, 0)),
                  pl.BlockSpec((1, L, d_inner), lambda b, c: (b, c, zb)),
                  pl.BlockSpec((1, L, G * N), lambda b, c: (b, c, bc0)),
                  pl.BlockSpec((1, L, G * N), lambda b, c: (b, c, bc0 + 1)),
                  pl.BlockSpec((1, L, 128), lambda b, c: (b, c, 2)),
                  pl.BlockSpec((SSM_CONV, cdim), lambda b, c: (0, 0)),
                  pl.BlockSpec((1, cdim), lambda b, c: (0, 0)),
                  pl.BlockSpec((1, H), lambda b, c: (0, 0)),
                  pl.BlockSpec((1, H), lambda b, c: (0, 0)),
                  pl.BlockSpec((1, d_inner), lambda b, c: (0, 0)),
                  pl.BlockSpec((1, d_inner), lambda b, c: (0, 0)),
                  pl.BlockSpec((H, d_inner), lambda b, c: (0, 0))],
        out_specs=pl.BlockSpec((1, L, d_inner), lambda b, c: (b, c, 0)),
        out_shape=jax.ShapeDtypeStruct((bsz, s, d_inner), BF16),
        scratch_shapes=[pltpu.VMEM((L + 8, cdim), F32),
                        pltpu.VMEM((G, N, gw), F32),
                        pltpu.VMEM((L, d_inner), F32)],
        compiler_params=_cparams(("arbitrary", "arbitrary")),
    )(proj, proj, proj, proj, small, conv_w, conv_b.reshape(1, cdim), dt_bias.reshape(1, H),
      a_log.reshape(1, H), jnp.repeat(d_skip, P).reshape(1, d_inner), norm_g.reshape(1, d_inner), e_mat)


def _pack_w_in(w_in):
    d = w_in.shape[0]
    d_inner = 2 * d
    gn = SSM_N_GROUPS * SSM_D_STATE
    n_h = d_inner // SSM_HEAD_DIM
    sizes = (d_inner, d_inner + 2 * gn, n_h, ATT_N_HEADS * ATT_HEAD_DIM, ATT_N_KV_HEADS * ATT_HEAD_DIM,
             ATT_N_KV_HEADS * ATT_HEAD_DIM, IDX_N_HEADS * IDX_HEAD_DIM, IDX_HEAD_DIM, IDX_N_HEADS, d, d)
    pts = [int(p) for p in np.cumsum(sizes)[:-1]]
    z, xbc, dt, q, k, v, qi, ki, wi, g_ssm, g_att = jnp.split(w_in, pts, axis=1)
    big = jnp.concatenate([z, xbc, q, g_ssm, g_att, k, v, qi], axis=1).astype(BF16)
    zeros = lambda n: jnp.zeros((d, n), w_in.dtype)
    small = jnp.concatenate([ki, zeros(128 - IDX_HEAD_DIM), wi, zeros(128 - IDX_N_HEADS), dt, zeros(128 - n_h)],
                            axis=1).astype(BF16)
    return big, small


def _rope_kernel(q_ref, k_ref, v_ref, qi_ref, sm_ref, wi_ref, ca_ref, sa_ref, ci_ref, si_ref,
                 qo_ref, ko_ref, vo_ref, qio_ref, kio_ref, wo_ref):
    ca = ca_ref[0]
    sa = sa_ref[0]
    ci = ci_ref[0]
    si = si_ref[0]
    att_scale = ATT_HEAD_DIM ** -0.5
    q = q_ref[0].astype(F32)
    for h in range(ATT_N_HEADS):
        x = q[:, h * 128:(h + 1) * 128]
        qo_ref[0, h] = ((x * ca + pltpu.roll(x, 64, 1) * sa) * att_scale).astype(qo_ref.dtype)
    k = k_ref[0].astype(F32)
    for h in range(ATT_N_KV_HEADS):
        x = k[:, h * 128:(h + 1) * 128]
        ko_ref[0, :, h * 128:(h + 1) * 128] = (x * ca + pltpu.roll(x, 64, 1) * sa).astype(ko_ref.dtype)
    lane = lax.broadcasted_iota(I32, ci.shape, 1)
    first_half = (lane % IDX_HEAD_DIM) < (IDX_HEAD_DIM // 2)

    def rot_i(x):
        sw = jnp.where(first_half, pltpu.roll(x, 128 - IDX_HEAD_DIM // 2, 1), pltpu.roll(x, IDX_HEAD_DIM // 2, 1))
        return x * ci + sw * si

    qi = qi_ref[0].astype(F32)
    for j in range(IDX_N_HEADS * IDX_HEAD_DIM // 128):
        qio_ref[0, j] = rot_i(qi[:, j * 128:(j + 1) * 128]).astype(qio_ref.dtype)
    ki = rot_i(sm_ref[0])
    kio_ref[0, :, 0:128] = ki.astype(kio_ref.dtype)
    kio_ref[0, :, 128:256] = pltpu.roll(ki, IDX_HEAD_DIM, 1).astype(kio_ref.dtype)
    wo_ref[0] = wi_ref[0] * (IDX_HEAD_DIM ** -0.5 * IDX_N_HEADS ** -0.5)
    v = v_ref[0]
    ones = jnp.ones((v.shape[0], 128), vo_ref.dtype)
    for h in range(ATT_N_KV_HEADS):
        vo_ref[0, :, h * 256:h * 256 + 128] = v[:, h * 128:(h + 1) * 128]
        vo_ref[0, :, h * 256 + 128:(h + 1) * 256] = ones


def _rope(proj, small, positions, d_model, tm):
    bsz, s, _ = proj.shape
    d_inner = 2 * d_model
    nq = ATT_N_HEADS * ATT_HEAD_DIM
    nkv = ATT_N_KV_HEADS * ATT_HEAD_DIM
    nqi = IDX_N_HEADS * IDX_HEAD_DIM
    q_off = 2 * d_inner + 2 * SSM_N_GROUPS * SSM_D_STATE
    k_off = q_off + nq + 2 * d_model
    qi_off = k_off + 2 * nkv

    def tables(dim):
        inv = ROPE_THETA ** (-jnp.arange(0, dim, 2, dtype=F32) / dim)
        ang = positions.astype(F32)[..., None] * inv
        return jnp.cos(ang), jnp.sin(ang)

    cos_a, sin_a = tables(ATT_HEAD_DIM)
    cos_i, sin_i = tables(IDX_HEAD_DIM)
    ca = jnp.concatenate([cos_a, cos_a], -1)
    sa = jnp.concatenate([-sin_a, sin_a], -1)
    ci = jnp.concatenate([cos_i] * 4, -1)
    si = jnp.concatenate([-sin_i, sin_i, -sin_i, sin_i], -1)
    tab = pl.BlockSpec((1, tm, 128), lambda b, i: (b, i, 0))
    return pl.pallas_call(
        _rope_kernel, name="rope",
        grid=(bsz, s // tm),
        in_specs=[pl.BlockSpec((1, tm, nq), lambda b, i: (b, i, q_off // nq)),
                  pl.BlockSpec((1, tm, nkv), lambda b, i: (b, i, k_off // nkv)),
                  pl.BlockSpec((1, tm, nkv), lambda b, i: (b, i, k_off // nkv + 1)),
                  pl.BlockSpec((1, tm, nqi), lambda b, i: (b, i, qi_off // nqi)),
                  pl.BlockSpec((1, tm, 128), lambda b, i: (b, i, 0)),
                  pl.BlockSpec((1, tm, 128), lambda b, i: (b, i, 1)),
                  tab, tab, tab, tab],
        out_specs=[pl.BlockSpec((1, ATT_N_HEADS, tm, 128), lambda b, i: (b, 0, i, 0)),
                   pl.BlockSpec((1, tm, nkv), lambda b, i: (b, i, 0)),
                   pl.BlockSpec((1, tm, 2 * nkv), lambda b, i: (b, i, 0)),
                   pl.BlockSpec((1, nqi // 128, tm, 128), lambda b, i: (b, 0, i, 0)),
                   pl.BlockSpec((1, tm, 256), lambda b, i: (b, i, 0)),
                   tab],
        out_shape=[jax.ShapeDtypeStruct((bsz, ATT_N_HEADS, s, 128), BF16),
                   jax.ShapeDtypeStruct((bsz, s, nkv), BF16),
                   jax.ShapeDtypeStruct((bsz, s, 2 * nkv), BF16),
                   jax.ShapeDtypeStruct((bsz, nqi // 128, s, 128), BF16),
                   jax.ShapeDtypeStruct((bsz, s, 256), BF16),
                   jax.ShapeDtypeStruct((bsz, s, 128), F32)],
        compiler_params=_cparams(("arbitrary", "arbitrary")),
    )(proj, proj, proj, proj, small, small, ca, sa, ci, si)


DSA_TQ_SEL = 128
DSA_TQ = 256
DSA_TK = 512
DSA_RC = 64


def _fill_head_weights(wb_ref, w):
    tq = w.shape[0]
    for h in range(IDX_N_HEADS):
        wb_ref[h] = jnp.broadcast_to(w[:, h:h + 1], (tq, 128))


def _score_keys(qi_pairs, wb_ref, k2, emit):
    n_pair, tq, _ = qi_pairs.shape
    lhs = qi_pairs.reshape(n_pair * tq, 128)
    nt = (((1,), (1,)), ((), ()))
    l_even = lax.dot_general(lhs, k2[:, 0:128], nt, preferred_element_type=F32)
    l_odd = lax.dot_general(lhs, k2[:, 128:256], nt, preferred_element_type=F32)
    for c in range(k2.shape[0] // 128):
        sc = None
        for j in range(n_pair):
            for l, h in ((l_even, 2 * j), (l_odd, 2 * j + 1)):
                t = jnp.maximum(l[j * tq:(j + 1) * tq, c * 128:(c + 1) * 128], 0.0) * wb_ref[h]
                sc = t if sc is None else sc + t
        bits = lax.bitcast_convert_type(sc, I32)
        bits = jnp.where(bits == INT_MIN, 0, bits)
        emit(c, bits ^ ((bits >> 31) & 0x7FFFFFFF))


def _thresh_kernel(qi_ref, w_ref, ki_ref, o_ref, keys_ref, wb_ref, *, topk, seq):
    tq, tk = DSA_TQ_SEL, DSA_TK
    nch = tk // 128
    i = pl.program_id(1)
    nkt = (i * tq + tq - 1) // tk + 1
    qi = qi_ref[0]
    _fill_head_weights(wb_ref, w_ref[0])
    qpos = i * tq + lax.broadcasted_iota(I32, (tq, 128), 0)
    lane = lax.broadcasted_iota(I32, (tq, 128), 1)

    def fill(t, carry):
        def emit(c, key):
            keys_ref[t, :, c * 128:(c + 1) * 128] = jnp.where(t * tk + c * 128 + lane <= qpos, key, INT_MIN)
        _score_keys(qi, wb_ref, ki_ref[0, pl.ds(pl.multiple_of(t * tk, tk), tk), :], emit)
        return carry

    lax.fori_loop(0, nkt, fill, 0)

    def count(pred):
        def body(t, acc):
            for c in range(nch):
                acc = acc + jnp.where(pred(keys_ref[t, :, c * 128:(c + 1) * 128], t * tk + c * 128), 1.0, 0.0)
            return acc
        acc = lax.fori_loop(0, nkt, body, jnp.zeros((tq, 128), F32))
        return jnp.sum(acc, axis=-1, keepdims=True)

    kf = float(topk)
    c0 = count(lambda k, base: k >= 0)
    thr = jnp.where(c0 >= kf, 0, INT_MIN).astype(I32)

    def bit_body(bi, thr):
        cand = thr | jnp.left_shift(jnp.int32(1), 30 - bi)
        cand_b = jnp.broadcast_to(cand, (tq, 128))
        c = count(lambda k, base: k >= cand_b)
        return jnp.where(c >= kf, cand, thr)

    thr = lax.fori_loop(0, 31, bit_body, thr)
    thr_b = jnp.broadcast_to(thr, (tq, 128))
    need = kf - count(lambda k, base: k > thr_b)
    n_eq = count(lambda k, base: k == thr_b)

    def emit_out(cut):
        o_ref[0] = jnp.where(lane == 0, thr, jnp.where(lane == 1, cut, 0))

    emit_out(jnp.full((tq, 1), seq, I32))
    partial = jnp.logical_and(n_eq > need, thr > INT_MIN)

    @pl.when(jnp.max(jnp.where(partial, 1.0, 0.0)) > 0.0)
    def _():
        def idx_body(bi, cut):
            cand = cut | jnp.left_shift(jnp.int32(1), (seq.bit_length() - 1) - bi)
            cand_b = jnp.broadcast_to(cand, (tq, 128))
            c = count(lambda k, base: jnp.logical_and(k == thr_b, base + lane < cand_b))
            return jnp.where(c < need, cand, cut)
        emit_out(lax.fori_loop(0, seq.bit_length(), idx_body, jnp.zeros((tq, 1), I32)))


def _thresh(qi_rot, w_s, ki_rot, topk):
    bsz, n_pair, s, _ = qi_rot.shape
    tq, tk = DSA_TQ_SEL, DSA_TK
    return pl.pallas_call(
        functools.partial(_thresh_kernel, topk=topk, seq=s), name="dsa_thresh",
        grid=(bsz, s // tq),
        in_specs=[pl.BlockSpec((1, n_pair, tq, 128), lambda b, i: (b, 0, i, 0)),
                  pl.BlockSpec((1, tq, 128), lambda b, i: (b, i, 0)),
                  pl.BlockSpec((1, s, 256), lambda b, i: (b, 0, 0))],
        out_specs=pl.BlockSpec((1, tq, 128), lambda b, i: (b, i, 0)),
        out_shape=jax.ShapeDtypeStruct((bsz, s, 128), I32),
        scratch_shapes=[pltpu.VMEM((s // tk, tq, tk), I32),
                        pltpu.VMEM((IDX_N_HEADS, tq, 128), F32)],
        compiler_params=_cparams(("arbitrary", "arbitrary")),
    )(qi_rot, w_s, ki_rot)


def _attn_kernel(qb_ref, kt_ref, q_ref, k_ref, v_ref, qi_ref, w_ref, ki_ref, thr_ref, o_ref,
                 m_ref, acc_ref, wb_ref, bias_ref, s_ref, p_ref, al_ref):
    tq, tk, rc = DSA_TQ, DSA_TK, DSA_RC
    rep = ATT_N_HEADS // ATT_N_KV_HEADS
    nch = tk // 128
    step = pl.program_id(1)
    qb = qb_ref[step]
    kt = kt_ref[step]

    @pl.when(kt == 0)
    def _():
        m_ref[...] = jnp.full(m_ref.shape, NEG_BIG, F32)
        acc_ref[...] = jnp.zeros(acc_ref.shape, F32)
        _fill_head_weights(wb_ref, w_ref[0])

    thr_blk = thr_ref[0]
    thr = jnp.broadcast_to(thr_blk[:, 0:1], (tq, 128))
    cut = jnp.broadcast_to(thr_blk[:, 1:2], (tq, 128))
    qpos = qb * tq + lax.broadcasted_iota(I32, (tq, 128), 0)
    lane = lax.broadcasted_iota(I32, (tq, 128), 1)

    def emit_bias(c, key):
        kpos = kt * tk + c * 128 + lane
        sel = jnp.logical_or(key > thr, jnp.logical_and(key == thr, kpos <= cut))
        bias_ref[:, c * 128:(c + 1) * 128] = jnp.where(jnp.logical_and(kpos <= qpos, sel), 0.0, NEG_BIG)

    _score_keys(qi_ref[0], wb_ref, ki_ref[0], emit_bias)

    for g in range(ATT_N_KV_HEADS):
        qg = q_ref[0, g * rep:(g + 1) * rep].reshape(rep * tq, ATT_HEAD_DIM)
        kg = k_ref[0, :, g * 128:(g + 1) * 128]
        s_ref[...] = lax.dot_general(qg, kg, (((1,), (1,)), ((), ())), preferred_element_type=F32)

        def chunk(ci, carry):
            rows = pl.ds(pl.multiple_of(ci * rc, rc), rc)
            brows = pl.ds(pl.multiple_of((ci % (tq // rc)) * rc, rc), rc)
            sc = [s_ref[rows, c * 128:(c + 1) * 128] + bias_ref[brows, c * 128:(c + 1) * 128] for c in range(nch)]
            mx = sc[0]
            for c in range(1, nch):
                mx = jnp.maximum(mx, sc[c])
            m_old = m_ref[g, rows, :]
            m_new = jnp.maximum(m_old, jnp.max(mx, axis=-1, keepdims=True))
            al_ref[rows, :] = jnp.exp(m_old - m_new)
            m_ref[g, rows, :] = m_new
            for c in range(nch):
                p_ref[rows, c * 128:(c + 1) * 128] = jnp.exp(sc[c] - m_new).astype(p_ref.dtype)
            return carry

        lax.fori_loop(0, rep * tq // rc, chunk, 0, unroll=True)
        pv = jnp.dot(p_ref[...], v_ref[0, :, g * 256:(g + 1) * 256], preferred_element_type=F32)
        al = al_ref[...]
        acc_ref[g, :, 0:128] = al * acc_ref[g, :, 0:128] + pv[:, 0:128]
        acc_ref[g, :, 128:256] = al * acc_ref[g, :, 128:256] + pv[:, 128:256]

    @pl.when(kt == (qb * tq + tq - 1) // tk)
    def _():
        for g in range(ATT_N_KV_HEADS):
            o = acc_ref[g, :, 0:128] / acc_ref[g, :, 128:256]
            for r in range(rep):
                h = g * rep + r
                o_ref[0, :, h * 128:(h + 1) * 128] = o[r * tq:(r + 1) * tq].astype(o_ref.dtype)


def _attn(q_rot, k_rot, v_ext, qi_rot, w_s, ki_rot, thr):
    bsz, n_h, s, _ = q_rot.shape
    tq, tk = DSA_TQ, DSA_TK
    nkv = ATT_N_KV_HEADS * ATT_HEAD_DIM
    n_pair = qi_rot.shape[1]
    pairs = [(qb, kt) for qb in range(s // tq) for kt in range((qb * tq + tq - 1) // tk + 1)]
    qb_arr = jnp.asarray([p[0] for p in pairs], I32)
    kt_arr = jnp.asarray([p[1] for p in pairs], I32)
    rep = n_h // ATT_N_KV_HEADS
    grid_spec = pltpu.PrefetchScalarGridSpec(
        num_scalar_prefetch=2,
        grid=(bsz, len(pairs)),
        in_specs=[pl.BlockSpec((1, n_h, tq, 128), lambda b, t, qb, kt: (b, 0, qb[t], 0)),
                  pl.BlockSpec((1, tk, nkv), lambda b, t, qb, kt: (b, kt[t], 0)),
                  pl.BlockSpec((1, tk, 2 * nkv), lambda b, t, qb, kt: (b, kt[t], 0)),
                  pl.BlockSpec((1, n_pair, tq, 128), lambda b, t, qb, kt: (b, 0, qb[t], 0)),
                  pl.BlockSpec((1, tq, 128), lambda b, t, qb, kt: (b, qb[t], 0)),
                  pl.BlockSpec((1, tk, 256), lambda b, t, qb, kt: (b, kt[t], 0)),
                  pl.BlockSpec((1, tq, 128), lambda b, t, qb, kt: (b, qb[t], 0))],
        out_specs=pl.BlockSpec((1, tq, n_h * 128), lambda b, t, qb, kt: (b, qb[t], 0)),
        scratch_shapes=[pltpu.VMEM((ATT_N_KV_HEADS, rep * tq, 128), F32),
                        pltpu.VMEM((ATT_N_KV_HEADS, rep * tq, 256), F32),
                        pltpu.VMEM((IDX_N_HEADS, tq, 128), F32),
                        pltpu.VMEM((tq, tk), F32),
                        pltpu.VMEM((rep * tq, tk), F32),
                        pltpu.VMEM((rep * tq, tk), BF16),
                        pltpu.VMEM((rep * tq, 128), F32)])
    return pl.pallas_call(
        _attn_kernel, name="dsa_attn",
        grid_spec=grid_spec,
        out_shape=jax.ShapeDtypeStruct((bsz, s, n_h * 128), BF16),
        compiler_params=_cparams(("arbitrary", "arbitrary")),
    )(qb_arr, kt_arr, q_rot, k_rot, v_ext, qi_rot, w_s, ki_rot, thr)


def _merge_kernel(y_ref, o_ref, gs_ref, ga_ref, ws_ref, wa_ref, out_ref):
    a = jnp.dot(y_ref[0], ws_ref[...], preferred_element_type=F32)
    b = jnp.dot(o_ref[0], wa_ref[...], preferred_element_type=F32)
    out_ref[0] = (_sigmoid(gs_ref[0].astype(F32)) * a + _sigmoid(ga_ref[0].astype(F32)) * b).astype(out_ref.dtype)


def _merge(y_ssm, o_att, proj, w_ssm, w_att, tm, tn):
    bsz, s, d_inner = y_ssm.shape
    d_att = o_att.shape[2]
    d = w_ssm.shape[1]
    gs_off = 2 * d_inner + 2 * SSM_N_GROUPS * SSM_D_STATE + d_att
    ga_off = gs_off + d
    return pl.pallas_call(
        _merge_kernel, name="merge",
        grid=(bsz, s // tm, d // tn),
        in_specs=[pl.BlockSpec((1, tm, d_inner), lambda b, i, j: (b, i, 0)),
                  pl.BlockSpec((1, tm, d_att), lambda b, i, j: (b, i, 0)),
                  pl.BlockSpec((1, tm, tn), lambda b, i, j: (b, i, gs_off // tn + j)),
                  pl.BlockSpec((1, tm, tn), lambda b, i, j: (b, i, ga_off // tn + j)),
                  pl.BlockSpec((d_inner, tn), lambda b, i, j: (0, j)),
                  pl.BlockSpec((d_att, tn), lambda b, i, j: (0, j))],
        out_specs=pl.BlockSpec((1, tm, tn), lambda b, i, j: (b, i, j)),
        out_shape=jax.ShapeDtypeStruct((bsz, s, d), BF16),
        compiler_params=_cparams(("arbitrary", "arbitrary", "arbitrary")),
    )(y_ssm, o_att, proj, proj, w_ssm, w_att)


def _post_kernel(m_ref, wo_ref, x_ref, mod_ref, g_ref, rw_ref, rb_ref, x1_ref, h2_ref, gate_ref, idx_ref):
    mod = mod_ref[0]
    out = jnp.dot(m_ref[0], wo_ref[...], preferred_element_type=F32)
    x1 = x_ref[0] + mod[2:3] * out
    x1_ref[0] = x1
    h2 = _modnorm(x1, g_ref[...], mod[4:5], mod[3:4])
    h2_ref[0] = h2
    logit = _dot_hi(h2, rw_ref[...]) + rb_ref[...]
    lane = lax.broadcasted_iota(I32, logit.shape, 1)
    lane_f = lane.astype(F32)
    vals, idxs = [], []
    for _ in range(TOP_K):
        m = jnp.max(logit, axis=-1, keepdims=True)
        idx = jnp.min(jnp.where(logit == m, lane_f, 128.0), axis=-1, keepdims=True)
        vals.append(m)
        idxs.append(idx)
        logit = jnp.where(lane_f == idx, -3e38, logit)
    es = [jnp.exp(v - vals[0]) for v in vals]
    denom = es[0] + es[1] + es[2] + es[3]
    gate = jnp.zeros(logit.shape, F32)
    sel = jnp.zeros(logit.shape, F32)
    for k in range(TOP_K):
        gate = jnp.where(lane == k, es[k] / denom, gate)
        sel = jnp.where(lane == k, idxs[k], sel)
    gate_ref[0] = gate
    idx_ref[0] = sel.astype(I32)


def _post(merged, w_o, x, mod, g, router_w, router_b, tm):
    bsz, s, d = x.shape
    n_e = router_w.shape[1]
    rw = jnp.zeros((d, 128), F32).at[:, :n_e].set(router_w.astype(F32))
    rb = jnp.full((1, 128), NEG_BIG, F32).at[0, :n_e].set(router_b.astype(F32))
    row = lambda w: pl.BlockSpec((1, tm, w), lambda b, i: (b, i, 0))
    return pl.pallas_call(
        _post_kernel, name="post_router",
        grid=(bsz, s // tm),
        in_specs=[row(d),
                  pl.BlockSpec((d, d), lambda b, i: (0, 0)),
                  row(d),
                  pl.BlockSpec((1, 6, d), lambda b, i: (b, 0, 0)),
                  pl.BlockSpec((1, d), lambda b, i: (0, 0)),
                  pl.BlockSpec((d, 128), lambda b, i: (0, 0)),
                  pl.BlockSpec((1, 128), lambda b, i: (0, 0))],
        out_specs=[row(d), row(d), row(128), row(128)],
        out_shape=[jax.ShapeDtypeStruct((bsz, s, d), F32),
                   jax.ShapeDtypeStruct((bsz, s, d), F32),
                   jax.ShapeDtypeStruct((bsz, s, 128), F32),
                   jax.ShapeDtypeStruct((bsz, s, 128), I32)],
        compiler_params=_cparams(("arbitrary", "arbitrary")),
    )(merged, w_o, x, mod, g.reshape(1, d), rw, rb)


MOE_TM = 512
MOE_TN = 512


def _moe_kernel(be_ref, nu_ref, rt_ref, rtn_ref, rd_ref, h_hbm, wg_ref, wu_ref, bg_ref, bu_ref, wd_ref, bd_ref,
                y_hbm, xf_ref, xb_ref, acc_ref, yo_ref, sem):
    tm = MOE_TM
    blk = pl.program_id(0)
    j = pl.program_id(1)
    n_j = pl.num_programs(1)
    n_used = nu_ref[0]

    def gather_start(tok_ref):
        def body(r, c):
            pltpu.make_async_copy(h_hbm.at[pl.ds(tok_ref[0, 0, r], 1)], xf_ref.at[pl.ds(r, 1)], sem.at[0]).start()
            return c
        lax.fori_loop(0, tm, body, 0, unroll=8)

    def all_rows_in():
        return pltpu.make_async_copy(h_hbm.at[pl.ds(0, tm)], xf_ref, sem.at[0])

    def all_rows_out():
        return pltpu.make_async_copy(yo_ref, y_hbm.at[pl.ds(0, tm)], sem.at[1])

    @pl.when(blk < n_used)
    def _():
        @pl.when(j == 0)
        def _():
            @pl.when(blk == 0)
            def _():
                gather_start(rt_ref)
            all_rows_in().wait()
            xb_ref[...] = xf_ref[...].astype(BF16)
            acc_ref[...] = jnp.zeros(acc_ref.shape, F32)

            @pl.when(blk + 1 < n_used)
            def _():
                gather_start(rtn_ref)

        x = xb_ref[...]
        gate = jnp.dot(x, wg_ref[0], preferred_element_type=F32) + bg_ref[0]
        up = jnp.dot(x, wu_ref[0], preferred_element_type=F32) + bu_ref[0]
        gate = jnp.minimum(gate, SWIGLU_LIMIT)
        up = jnp.clip(up, -SWIGLU_LIMIT, SWIGLU_LIMIT)
        act = (up + 1.0) * (gate * _sigmoid(SWIGLU_ALPHA * gate))
        acc_ref[...] += jnp.dot(act.astype(BF16), wd_ref[0], preferred_element_type=F32)

        @pl.when(j == n_j - 1)
        def _():
            @pl.when(blk > 0)
            def _():
                all_rows_out().wait()

            yo_ref[...] = acc_ref[...] + bd_ref[0]

            def body(r, c):
                pltpu.make_async_copy(yo_ref.at[pl.ds(r, 1)], y_hbm.at[pl.ds(rd_ref[0, 0, r], 1)], sem.at[1]).start()
                return c
            lax.fori_loop(0, tm, body, 0, unroll=8)

            @pl.when(blk == n_used - 1)
            def _():
                all_rows_out().wait()

    @pl.when(jnp.logical_and(blk >= n_used, j == 0))
    def _():
        @pl.when(blk == n_used)
        def _():
            yo_ref[...] = jnp.zeros(yo_ref.shape, F32)
        fill = pltpu.make_async_copy(yo_ref, y_hbm.at[pl.ds(pl.multiple_of(blk * tm, tm), tm)], sem.at[1])
        fill.start()
        fill.wait()


def _moe(h2, gate_idx, w_gu, b_gu, w_dn, b_dn):
    n_tok, d = h2.shape
    n_e, _, d2 = w_gu.shape
    d_e = d2 // 2
    tm, tn = MOE_TM, MOE_TN
    n_j = d_e // tn
    n_asg = n_tok * TOP_K
    e_flat = gate_idx[:, :TOP_K].reshape(-1)
    onehot = (e_flat[:, None] == jnp.arange(n_e, dtype=I32)[None, :]).astype(I32)
    csum = jnp.cumsum(onehot, axis=0)
    rank = jnp.take_along_axis(csum, e_flat[:, None], axis=1)[:, 0] - 1
    counts = csum[-1]
    padded = (counts + tm - 1) // tm * tm
    ends = jnp.cumsum(padded)
    dest = (ends - padded)[e_flat] + rank
    n_blocks = -(-(n_asg + n_e * (tm - 1)) // tm)
    row_asg = jnp.full((n_blocks * tm,), -1, I32).at[dest].set(jnp.arange(n_asg, dtype=I32))
    is_pad = row_asg < 0
    pad_rank = jnp.cumsum(is_pad.astype(I32)) - 1
    row_tok = jnp.where(is_pad, 0, row_asg // TOP_K).reshape(n_blocks, 1, tm)
    row_dst = jnp.where(is_pad, n_asg + pad_rank, (row_asg % TOP_K) * n_tok + row_asg // TOP_K)
    row_dst = row_dst.reshape(n_blocks, 1, tm)
    block_expert = jnp.minimum(
        jnp.searchsorted(ends, jnp.arange(n_blocks, dtype=I32) * tm, side='right'), n_e - 1).astype(I32)
    n_used = (ends[-1:] // tm).astype(I32)

    def jj(blk, j, nu):
        return jnp.where(blk < nu[0], j, n_j - 1)

    smem_rows = pl.BlockSpec((1, 1, tm), lambda blk, j, be, nu: (blk, 0, 0), memory_space=pltpu.SMEM)
    smem_next = pl.BlockSpec((1, 1, tm), lambda blk, j, be, nu: (jnp.minimum(blk + 1, n_blocks - 1), 0, 0),
                             memory_space=pltpu.SMEM)
    grid_spec = pltpu.PrefetchScalarGridSpec(
        num_scalar_prefetch=2,
        grid=(n_blocks, n_j),
        in_specs=[smem_rows, smem_next, smem_rows,
                  pl.BlockSpec(memory_space=pl.ANY),
                  pl.BlockSpec((1, d, tn), lambda blk, j, be, nu: (be[blk], 0, jj(blk, j, nu))),
                  pl.BlockSpec((1, d, tn), lambda blk, j, be, nu: (be[blk], 0, n_j + jj(blk, j, nu))),
                  pl.BlockSpec((1, 1, tn), lambda blk, j, be, nu: (be[blk], 0, jj(blk, j, nu))),
                  pl.BlockSpec((1, 1, tn), lambda blk, j, be, nu: (be[blk], 0, n_j + jj(blk, j, nu))),
                  pl.BlockSpec((1, tn, d), lambda blk, j, be, nu: (be[blk], jj(blk, j, nu), 0)),
                  pl.BlockSpec((1, 1, d), lambda blk, j, be, nu: (be[blk], 0, 0))],
        out_specs=pl.BlockSpec(memory_space=pl.ANY),
        scratch_shapes=[pltpu.VMEM((tm, d), F32),
                        pltpu.VMEM((tm, d), BF16),
                        pltpu.VMEM((tm, d), F32),
                        pltpu.VMEM((tm, d), F32),
                        pltpu.SemaphoreType.DMA((2,))])
    return pl.pallas_call(
        _moe_kernel, name="moe_experts",
        grid_spec=grid_spec,
        out_shape=jax.ShapeDtypeStruct((n_blocks * tm, d), F32),
        compiler_params=_cparams(("arbitrary", "arbitrary")),
    )(block_expert, n_used, row_tok, row_tok, row_dst, h2, w_gu, w_gu, b_gu.reshape(n_e, 1, d2),
      b_gu.reshape(n_e, 1, d2), w_dn, b_dn.reshape(n_e, 1, d))


def _combine_kernel(x1_ref, y0_ref, y1_ref, y2_ref, y3_ref, gate_ref, mod_ref, g_ref, o_ref, *, final):
    gate = gate_ref[0]
    moe = gate[:, 0:1] * y0_ref[...]
    for k, y_ref in ((1, y1_ref), (2, y2_ref), (3, y3_ref)):
        moe = moe + gate[:, k:k + 1] * y_ref[...]
    x2 = x1_ref[0] + mod_ref[0][5:6] * moe
    if final:
        ms = jnp.mean(x2 * x2, axis=-1, keepdims=True)
        x2 = x2 * lax.rsqrt(ms + NORM_EPS) * g_ref[...]
    o_ref[0] = x2


def _combine(x1, y, gates, mod, g, final, tm):
    bsz, s, d = x1.shape
    assert TOP_K == 4
    row = lambda w: pl.BlockSpec((1, tm, w), lambda b, i: (b, i, 0))
    nb = s // tm

    def slot(k):
        return pl.BlockSpec((tm, d), lambda b, i: (k * bsz * nb + b * nb + i, 0))

    return pl.pallas_call(
        functools.partial(_combine_kernel, final=final), name="combine",
        grid=(bsz, nb),
        in_specs=[row(d), slot(0), slot(1), slot(2), slot(3), row(128),
                  pl.BlockSpec((1, 6, d), lambda b, i: (b, 0, 0)),
                  pl.BlockSpec((1, d), lambda b, i: (0, 0))],
        out_specs=row(d),
        out_shape=jax.ShapeDtypeStruct((bsz, s, d), F32),
        compiler_params=_cparams(("arbitrary", "arbitrary")),
    )(x1, y, y, y, y, gates, mod, g.reshape(1, d))


def _pick(n, *cands):
    for c in cands:
        if n % c == 0:
            return c
    return n


def kernel(x, c, positions, ada_w, ada_b, norm_mix_g, w_in, conv_w, conv_b, dt_bias, a_log, d_skip, ssm_norm_g,
           w_out_ssm, w_out_att, w_o, norm_ffn_g, router_w, router_b, expert_w_gate_up, expert_b_gate_up,
           expert_w_down, expert_b_down, final_norm_g):
    bsz, s, d = x.shape
    depth = ada_w.shape[0]
    topk = min(IDX_TOPK_MAX, s // 4)
    for l in range(depth):
        mod = _adaln(c, ada_w[l], ada_b[l]).reshape(bsz, 6, d)
        w_big, w_small = _pack_w_in(w_in[l])
        tm = _pick(s, 1024, 512, 256)
        proj = _inproj(x, norm_mix_g[l], mod, w_big, BF16, tm, 1280)
        small = _inproj(x, norm_mix_g[l], mod, w_small, F32, tm, w_small.shape[1])
        y_ssm = _ssd(proj, small, conv_w[l], conv_b[l], dt_bias[l], a_log[l], d_skip[l], ssm_norm_g[l], 2 * d)
        q_rot, k_rot, v_ext, qi_rot, ki_rot, w_s = _rope(proj, small, positions, d, _pick(s, 512, 256))
        thr = _thresh(qi_rot, w_s, ki_rot, topk)
        o_att = _attn(q_rot, k_rot, v_ext, qi_rot, w_s, ki_rot, thr)
        merged = _merge(y_ssm, o_att, proj, w_out_ssm[l].astype(BF16), w_out_att[l].astype(BF16),
                        _pick(s, 512, 256), 1024)
        x1, h2, gates, gate_idx = _post(merged, w_o[l].astype(BF16), x, mod, norm_ffn_g[l], router_w[l],
                                        router_b[l], 256)
        y = _moe(h2.reshape(bsz * s, d), gate_idx.reshape(bsz * s, 128), expert_w_gate_up[l].astype(BF16),
                 expert_b_gate_up[l], expert_w_down[l].astype(BF16), expert_b_down[l])
        x = _combine(x1, y, gates, mod, final_norm_g, l == depth - 1, 256)
    return x
```

```python
import functools

import jax
import jax.numpy as jnp
import numpy as np
from jax import lax
from jax.experimental import pallas as pl
from jax.experimental.pallas import tpu as pltpu

F32 = jnp.float32
BF16 = jnp.bfloat16
I32 = jnp.int32
I16 = jnp.int16

SSM_HEAD_DIM = 64
SSM_N_GROUPS = 8
SSM_D_STATE = 128
SSM_CONV = 4
SSM_CHUNK = 128
ATT_N_HEADS = 16
ATT_N_KV_HEADS = 4
ATT_HEAD_DIM = 128
IDX_N_HEADS = 8
IDX_HEAD_DIM = 64
IDX_TOPK_MAX = 256
ROPE_THETA = 10000.0
N_EXPERTS = 32
TOP_K = 4
SWIGLU_ALPHA = 1.702
SWIGLU_LIMIT = 7.0
NORM_EPS = 1e-6

INT_MIN = -2147483648
NEG_BIG = -1e30
VMEM_LIMIT = 56 * 1024 * 1024


def _cparams(sem):
    return pltpu.CompilerParams(dimension_semantics=sem, vmem_limit_bytes=VMEM_LIMIT)


def _split3(a):
    hi = a.astype(BF16)
    r1 = a - hi.astype(F32)
    mid = r1.astype(BF16)
    lo = (r1 - mid.astype(F32)).astype(BF16)
    return hi, mid, lo


def _dot_x3(a, b_bf16, dims=None):
    out = None
    for p in _split3(a):
        if dims is None:
            t = jnp.dot(p, b_bf16, preferred_element_type=F32)
        else:
            t = lax.dot_general(p, b_bf16, dims, preferred_element_type=F32)
        out = t if out is None else out + t
    return out


def _dot_hi(a, b):
    a0, a1, a2 = _split3(a)
    b0, b1, b2 = _split3(b)
    d = functools.partial(jnp.dot, preferred_element_type=F32)
    return (d(a0, b0) + (d(a0, b1) + d(a1, b0))
            + (d(a0, b2) + d(a2, b0) + d(a1, b1)))


def _sigmoid(x):
    return 1.0 / (1.0 + jnp.exp(-x))


def _silu(x):
    return x * _sigmoid(x)


def _adaln_kernel(c_ref, w_ref, b_ref, o_ref):
    c = c_ref[...]
    o_ref[...] = _dot_hi(_silu(c), w_ref[...]) + b_ref[...]


def _adaln(c, ada_w, ada_b):
    bsz, d = c.shape
    n = ada_w.shape[1]
    rows = 8
    cp = jnp.zeros((rows, d), F32).at[:bsz].set(c)
    tn = 1024
    out = pl.pallas_call(
        _adaln_kernel, name="adaln",
        grid=(n // tn,),
        in_specs=[pl.BlockSpec((rows, d), lambda j: (0, 0)),
                  pl.BlockSpec((d, tn), lambda j: (0, j)),
                  pl.BlockSpec((1, tn), lambda j: (0, j))],
        out_specs=pl.BlockSpec((rows, tn), lambda j: (0, j)),
        out_shape=jax.ShapeDtypeStruct((rows, n), F32),
        compiler_params=_cparams(("arbitrary",)),
    )(cp, ada_w, ada_b.reshape(1, n))
    return out[:bsz]


def _modnorm(x, g, sc, sh):
    ms = jnp.mean(x * x, axis=-1, keepdims=True)
    y = x * lax.rsqrt(ms + NORM_EPS) * g
    return y * (1.0 + sc) + sh


def _inproj_kernel(x_ref, g_ref, mod_ref, w_ref, o_ref, h_ref, *, sc_row, sh_row):
    @pl.when(pl.program_id(2) == 0)
    def _():
        m = mod_ref[0]
        h = _modnorm(x_ref[0], g_ref[...], m[sc_row:sc_row + 1], m[sh_row:sh_row + 1])
        h_ref[...] = h.astype(BF16)

    o_ref[0] = jnp.dot(h_ref[...], w_ref[...], preferred_element_type=F32).astype(o_ref.dtype)


def _inproj(x, g, mod, w_bf16, out_dtype, tm, tn):
    bsz, s, d = x.shape
    n = w_bf16.shape[1]
    return pl.pallas_call(
        functools.partial(_inproj_kernel, sc_row=1, sh_row=0), name="inproj",
        grid=(bsz, s // tm, n // tn),
        in_specs=[pl.BlockSpec((1, tm, d), lambda b, i, j: (b, i, 0)),
                  pl.BlockSpec((1, d), lambda b, i, j: (0, 0)),
                  pl.BlockSpec((1, 6, d), lambda b, i, j: (b, 0, 0)),
                  pl.BlockSpec((d, tn), lambda b, i, j: (0, j))],
        out_specs=pl.BlockSpec((1, tm, tn), lambda b, i, j: (b, i, j)),
        out_shape=jax.ShapeDtypeStruct((bsz, s, n), out_dtype),
        scratch_shapes=[pltpu.VMEM((tm, d), BF16)],
        compiler_params=_cparams(("arbitrary", "arbitrary", "arbitrary")),
    )(x, g.reshape(1, d), mod, w_bf16)


def _ssd_kernel(z_ref, xs_ref, bm_ref, cm_ref, dt_ref, cw_ref, cb_ref, dtb_ref, alog_ref,
                dsk_ref, ng_ref, e_ref, o_ref, ubuf_ref, state_ref, y_ref):
    L = SSM_CHUNK
    G = SSM_N_GROUPS
    N = SSM_D_STATE
    P = SSM_HEAD_DIM
    d_inner = xs_ref.shape[2]
    H = d_inner // P
    R = H // G
    GW = R * P
    c_idx = pl.program_id(1)

    @pl.when(c_idx == 0)
    def _():
        ubuf_ref[0:8, :] = jnp.zeros((8, ubuf_ref.shape[1]), F32)
        state_ref[...] = jnp.zeros(state_ref.shape, F32)

    ubuf_ref[8:8 + L, 0:d_inner] = xs_ref[0].astype(F32)
    ubuf_ref[8:8 + L, d_inner:d_inner + G * N] = bm_ref[0].astype(F32)
    ubuf_ref[8:8 + L, d_inner + G * N:] = cm_ref[0].astype(F32)
    conv = cb_ref[...]
    for j in range(SSM_CONV):
        conv = conv + cw_ref[j:j + 1, :] * ubuf_ref[5 + j:5 + j + L, :]
    ubuf_ref[0:8, :] = ubuf_ref[L:L + 8, :]
    conv = _silu(conv)
    xs = conv[:, :d_inner]
    bmat = conv[:, d_inner:d_inner + G * N].astype(BF16)
    cmat = conv[:, d_inner + G * N:].astype(BF16)

    dtx = dt_ref[0][:, :H] + dtb_ref[...]
    dt = jnp.maximum(dtx, 0.0) + jnp.log(1.0 + jnp.exp(-jnp.abs(dtx)))
    a_neg = -jnp.exp(alog_ref[...])
    d_a = dt * a_neg

    row = lax.broadcasted_iota(I32, (L, L), 0)
    col = lax.broadcasted_iota(I32, (L, L), 1)
    causal = col <= row
    lmat = jnp.where(causal, 1.0, 0.0).astype(BF16)
    acum = None
    for p in _split3(d_a):
        t = jnp.dot(lmat, p, preferred_element_type=F32)
        acum = t if acum is None else acum + t
    acum_t = jnp.transpose(acum)

    e_mat = e_ref[...]
    dt_e = _dot_x3(dt, e_mat)
    acum_e = _dot_x3(acum, e_mat)
    exp_a_e = jnp.exp(acum_e)
    to_end_e = jnp.exp(acum_e[L - 1:L, :] - acum_e)

    xdt = xs * dt_e
    xdt_b = xdt.astype(BF16)
    xend_b = (xdt * to_end_e).astype(BF16)

    for g in range(G):
        b_g = bmat[:, g * N:(g + 1) * N]
        c_g = cmat[:, g * N:(g + 1) * N]
        cb = lax.dot_general(c_g, b_g, (((1,), (1,)), ((), ())), preferred_element_type=F32)
        st = state_ref[g]
        y_off = jnp.dot(c_g, st.astype(BF16), preferred_element_type=F32)
        y_ref[:, g * GW:(g + 1) * GW] = y_off * exp_a_e[:, g * GW:(g + 1) * GW]
        upd = lax.dot_general(b_g, xend_b[:, g * GW:(g + 1) * GW], (((0,), (0,)), ((), ())),
                              preferred_element_type=F32)
        state_ref[g] = st * exp_a_e[L - 1:L, g * GW:(g + 1) * GW] + upd
        for r in range(R):
            h = g * R + r
            seg = acum[:, h:h + 1] - acum_t[h:h + 1, :]
            decay = jnp.where(causal, jnp.exp(jnp.minimum(seg, 0.0)), 0.0)
            m = (cb * decay).astype(BF16)
            y_h = jnp.dot(m, xdt_b[:, h * P:(h + 1) * P], preferred_element_type=F32)
            y_ref[:, h * P:(h + 1) * P] += y_h

    y = y_ref[...] + xs * dsk_ref[...]
    yg = y * _silu(z_ref[0].astype(F32))
    for g in range(G):
        blk = yg[:, g * GW:(g + 1) * GW]
        ms = jnp.mean(blk * blk, axis=-1, keepdims=True)
        o_ref[0, :, g * GW:(g + 1) * GW] = (
            blk * lax.rsqrt(ms + NORM_EPS) * ng_ref[:, g * GW:(g + 1) * GW]).astype(o_ref.dtype)


def _ssd(proj, small, conv_w, conv_b, dt_bias, a_log, d_skip, norm_g, d_inner):
    bsz, s, _ = proj.shape
    L = SSM_CHUNK
    G, N, P = SSM_N_GROUPS, SSM_D_STATE, SSM_HEAD_DIM
    H = d_inner // P
    cdim = d_inner + 2 * G * N
    gw = d_inner // G
    zb = d_inner // d_inner
    bc0 = 2 * d_inner // (G * N)
    e_mat = (jnp.arange(d_inner, dtype=I32)[None, :] // P == jnp.arange(H, dtype=I32)[:, None]).astype(BF16)
    return pl.pallas_call(
        _ssd_kernel, name="ssd",
        grid=(bsz, s // L),
        in_specs=[pl.BlockSpec((1, L, d_inner), lambda b, c: (b, c, 0)),
                  pl.BlockSpec((1, L, d_inner), lambda b, c: (b, c, zb)),
                  pl.BlockSpec((1, L, G * N), lambda b, c: (b, c, bc0)),
                  pl.BlockSpec((1, L, G * N), lambda b, c: (b, c, bc0 + 1)),
                  pl.BlockSpec((1, L, 128), lambda b, c: (b, c, 2)),
                  pl.BlockSpec((SSM_CONV, cdim), lambda b, c: (0, 0)),
                  pl.BlockSpec((1, cdim), lambda b, c: (0, 0)),
                  pl.BlockSpec((1, H), lambda b, c: (0, 0)),
                  pl.BlockSpec((1, H), lambda b, c: (0, 0)),
                  pl.BlockSpec((1, d_inner), lambda b, c: (0, 0)),
                  pl.BlockSpec((1, d_inner), lambda b, c: (0, 0)),
                  pl.BlockSpec((H, d_inner), lambda b, c: (0, 0))],
        out_specs=pl.BlockSpec((1, L, d_inner), lambda b, c: (b, c, 0)),
        out_shape=jax.ShapeDtypeStruct((bsz, s, d_inner), BF16),
        scratch_shapes=[pltpu.VMEM((L + 8, cdim), F32),
                        pltpu.VMEM((G, N, gw), F32),
                        pltpu.VMEM((L, d_inner), F32)],
        compiler_params=_cparams(("arbitrary", "arbitrary")),
    )(proj, proj, proj, proj, small, conv_w, conv_b.reshape(1, cdim), dt_bias.reshape(1, H),
      a_log.reshape(1, H), jnp.repeat(d_skip, P).reshape(1, d_inner), norm_g.reshape(1, d_inner), e_mat)


def _pack_w_in(w_in):
    d = w_in.shape[0]
    d_inner = 2 * d
    gn = SSM_N_GROUPS * SSM_D_STATE
    n_h = d_inner // SSM_HEAD_DIM
    sizes = (d_inner, d_inner + 2 * gn, n_h, ATT_N_HEADS * ATT_HEAD_DIM, ATT_N_KV_HEADS * ATT_HEAD_DIM,
             ATT_N_KV_HEADS * ATT_HEAD_DIM, IDX_N_HEADS * IDX_HEAD_DIM, IDX_HEAD_DIM, IDX_N_HEADS, d, d)
    pts = [int(p) for p in np.cumsum(sizes)[:-1]]
    z, xbc, dt, q, k, v, qi, ki, wi, g_ssm, g_att = jnp.split(w_in, pts, axis=1)
    big = jnp.concatenate([z, xbc, q, g_ssm, g_att, k, v, qi], axis=1).astype(BF16)
    zeros = lambda n: jnp.zeros((d, n), w_in.dtype)
    small = jnp.concatenate([ki, zeros(128 - IDX_HEAD_DIM), wi, zeros(128 - IDX_N_HEADS), dt, zeros(128 - n_h)],
                            axis=1).astype(BF16)
    return big, small


def _rope_kernel(q_ref, k_ref, v_ref, qi_ref, sm_ref, wi_ref, ca_ref, sa_ref, ci_ref, si_ref,
                 qo_ref, ko_ref, vo_ref, qio_ref, kio_ref, wo_ref):
    ca = ca_ref[0]
    sa = sa_ref[0]
    ci = ci_ref[0]
    si = si_ref[0]
    att_scale = ATT_HEAD_DIM ** -0.5 * 1.4426950408889634
    q = q_ref[0].astype(F32)
    for h in range(ATT_N_HEADS):
        x = q[:, h * 128:(h + 1) * 128]
        qo_ref[0, h] = ((x * ca + pltpu.roll(x, 64, 1) * sa) * att_scale).astype(qo_ref.dtype)
    k = k_ref[0].astype(F32)
    for h in range(ATT_N_KV_HEADS):
        x = k[:, h * 128:(h + 1) * 128]
        ko_ref[0, :, h * 128:(h + 1) * 128] = (x * ca + pltpu.roll(x, 64, 1) * sa).astype(ko_ref.dtype)
    lane = lax.broadcasted_iota(I32, ci.shape, 1)
    first_half = (lane % IDX_HEAD_DIM) < (IDX_HEAD_DIM // 2)

    def rot_i(x):
        sw = jnp.where(first_half, pltpu.roll(x, 128 - IDX_HEAD_DIM // 2, 1), pltpu.roll(x, IDX_HEAD_DIM // 2, 1))
        return x * ci + sw * si

    qi = qi_ref[0].astype(F32)
    for j in range(IDX_N_HEADS * IDX_HEAD_DIM // 128):
        qio_ref[0, j] = rot_i(qi[:, j * 128:(j + 1) * 128]).astype(qio_ref.dtype)
    ki = rot_i(sm_ref[0])
    kio_ref[0, :, 0:128] = ki.astype(kio_ref.dtype)
    kio_ref[0, :, 128:256] = pltpu.roll(ki, IDX_HEAD_DIM, 1).astype(kio_ref.dtype)
    wo_ref[0] = wi_ref[0] * (IDX_HEAD_DIM ** -0.5 * IDX_N_HEADS ** -0.5)
    v = v_ref[0]
    ones = jnp.ones((v.shape[0], 128), vo_ref.dtype)
    for h in range(ATT_N_KV_HEADS):
        vo_ref[0, :, h * 256:h * 256 + 128] = v[:, h * 128:(h + 1) * 128]
        vo_ref[0, :, h * 256 + 128:(h + 1) * 256] = ones


def _rope(proj, small, positions, d_model, tm):
    bsz, s, _ = proj.shape
    d_inner = 2 * d_model
    nq = ATT_N_HEADS * ATT_HEAD_DIM
    nkv = ATT_N_KV_HEADS * ATT_HEAD_DIM
    nqi = IDX_N_HEADS * IDX_HEAD_DIM
    q_off = 2 * d_inner + 2 * SSM_N_GROUPS * SSM_D_STATE
    k_off = q_off + nq + 2 * d_model
    qi_off = k_off + 2 * nkv

    def tables(dim):
        inv = ROPE_THETA ** (-jnp.arange(0, dim, 2, dtype=F32) / dim)
        ang = positions.astype(F32)[..., None] * inv
        return jnp.cos(ang), jnp.sin(ang)

    cos_a, sin_a = tables(ATT_HEAD_DIM)
    cos_i, sin_i = tables(IDX_HEAD_DIM)
    ca = jnp.concatenate([cos_a, cos_a], -1)
    sa = jnp.concatenate([-sin_a, sin_a], -1)
    ci = jnp.concatenate([cos_i] * 4, -1)
    si = jnp.concatenate([-sin_i, sin_i, -sin_i, sin_i], -1)
    tab = pl.BlockSpec((1, tm, 128), lambda b, i: (b, i, 0))
    return pl.pallas_call(
        _rope_kernel, name="rope",
        grid=(bsz, s // tm),
        in_specs=[pl.BlockSpec((1, tm, nq), lambda b, i: (b, i, q_off // nq)),
                  pl.BlockSpec((1, tm, nkv), lambda b, i: (b, i, k_off // nkv)),
                  pl.BlockSpec((1, tm, nkv), lambda b, i: (b, i, k_off // nkv + 1)),
                  pl.BlockSpec((1, tm, nqi), lambda b, i: (b, i, qi_off // nqi)),
                  pl.BlockSpec((1, tm, 128), lambda b, i: (b, i, 0)),
                  pl.BlockSpec((1, tm, 128), lambda b, i: (b, i, 1)),
                  tab, tab, tab, tab],
        out_specs=[pl.BlockSpec((1, ATT_N_HEADS, tm, 128), lambda b, i: (b, 0, i, 0)),
                   pl.BlockSpec((1, tm, nkv), lambda b, i: (b, i, 0)),
                   pl.BlockSpec((1, tm, 2 * nkv), lambda b, i: (b, i, 0)),
                   pl.BlockSpec((1, nqi // 128, tm, 128), lambda b, i: (b, 0, i, 0)),
                   pl.BlockSpec((1, tm, 256), lambda b, i: (b, i, 0)),
                   tab],
        out_shape=[jax.ShapeDtypeStruct((bsz, ATT_N_HEADS, s, 128), BF16),
                   jax.ShapeDtypeStruct((bsz, s, nkv), BF16),
                   jax.ShapeDtypeStruct((bsz, s, 2 * nkv), BF16),
                   jax.ShapeDtypeStruct((bsz, nqi // 128, s, 128), BF16),
                   jax.ShapeDtypeStruct((bsz, s, 256), BF16),
                   jax.ShapeDtypeStruct((bsz, s, 128), F32)],
        compiler_params=_cparams(("arbitrary", "arbitrary")),
    )(proj, proj, proj, proj, small, small, ca, sa, ci, si)


DSA_TQ_SEL = 128
DSA_TQ = 256
DSA_TK = 512
DSA_RC = 64


def _fill_head_weights(wb_ref, w):
    tq = w.shape[0]
    for h in range(IDX_N_HEADS):
        wb_ref[h] = jnp.broadcast_to(w[:, h:h + 1], (tq, 128))


def _score_keys(qi_pairs, wb_ref, k2, emit):
    n_pair, tq, _ = qi_pairs.shape
    lhs = qi_pairs.reshape(n_pair * tq, 128)
    nt = (((1,), (1,)), ((), ()))
    l_even = lax.dot_general(lhs, k2[:, 0:128], nt, preferred_element_type=F32)
    l_odd = lax.dot_general(lhs, k2[:, 128:256], nt, preferred_element_type=F32)
    for c in range(k2.shape[0] // 128):
        sc = None
        for j in range(n_pair):
            for l, h in ((l_even, 2 * j), (l_odd, 2 * j + 1)):
                t = jnp.maximum(l[j * tq:(j + 1) * tq, c * 128:(c + 1) * 128], 0.0) * wb_ref[h]
                sc = t if sc is None else sc + t
        bits = lax.bitcast_convert_type(sc, I32)
        bits = jnp.where(bits == INT_MIN, 0, bits)
        emit(c, bits ^ ((bits >> 31) & 0x7FFFFFFF))


def _select_kernel(qi_ref, w_ref, ki_ref, mask_hbm, keys_ref, h16_ref, wb_ref, cut_ref, stage_ref, sem,
                   *, topk, seq):
    tq, tk = DSA_TQ_SEL, DSA_TK
    nch = tk // 128
    nt = seq // tk
    b = pl.program_id(0)
    i = pl.program_id(1)
    nkt = (i * tq + tq - 1) // tk + 1
    qi = qi_ref[0]
    _fill_head_weights(wb_ref, w_ref[0])
    qpos = i * tq + lax.broadcasted_iota(I32, (tq, 128), 0)
    lane = lax.broadcasted_iota(I32, (tq, 128), 1)
    chunk = lambda c: slice(c * 128, (c + 1) * 128)

    def fill(t, carry):
        def emit(c, key):
            key = jnp.where(t * tk + c * 128 + lane <= qpos, key, INT_MIN)
            keys_ref[t, :, chunk(c)] = key
            h16_ref[t, :, chunk(c)] = (key >> 16).astype(I16)
        _score_keys(qi, wb_ref, ki_ref[0, pl.ds(pl.multiple_of(t * tk, tk), tk), :], emit)
        return carry

    lax.fori_loop(0, nkt, fill, 0)
    kf = float(topk)

    def search16():
        def count16(cand):
            def body(t, acc):
                for c in range(nch):
                    acc = acc + jnp.where(h16_ref[t, :, chunk(c)] >= cand, jnp.int16(1), jnp.int16(0))
                return acc
            acc = lax.fori_loop(0, nkt, body, jnp.zeros((tq, 128), I16))
            return jnp.sum(acc.astype(F32), axis=-1, keepdims=True)

        v = jnp.where(count16(jnp.zeros((tq, 128), I16)) >= kf, 0, -32768).astype(I32)

        def bit_body(bi, v):
            cand = v | jnp.left_shift(jnp.int32(1), 14 - bi)
            c = count16(jnp.broadcast_to(cand, (tq, 128)).astype(I16))
            return jnp.where(c >= kf, cand, v)
        return lax.fori_loop(0, 15, bit_body, v)

    thr_hi = search16()
    hi_b = jnp.broadcast_to(thr_hi, (tq, 128))

    def lower_half(t, carry):
        for c in range(nch):
            k = keys_ref[t, :, chunk(c)]
            hi = k >> 16
            lo = (k & 0xFFFF) - 32768
            h16_ref[t, :, chunk(c)] = jnp.where(hi == hi_b, lo, jnp.where(hi > hi_b, 32767, -32768)).astype(I16)
        return carry

    lax.fori_loop(0, nkt, lower_half, 0)
    thr = jnp.left_shift(thr_hi, 16) + (search16() + 32768)
    thr_b = jnp.broadcast_to(thr, (tq, 128))

    def count(pred):
        def body(t, acc):
            for c in range(nch):
                acc = acc + jnp.where(pred(keys_ref[t, :, chunk(c)], t * tk + c * 128), 1.0, 0.0)
            return acc
        acc = lax.fori_loop(0, nkt, body, jnp.zeros((tq, 128), F32))
        return jnp.sum(acc, axis=-1, keepdims=True)

    need = kf - count(lambda k, base: k > thr_b)
    n_eq = count(lambda k, base: k == thr_b)
    cut_ref[...] = jnp.full((tq, 128), seq, I32)
    partial = jnp.logical_and(n_eq > need, thr > INT_MIN)

    @pl.when(jnp.max(jnp.where(partial, 1.0, 0.0)) > 0.0)
    def _():
        def idx_body(bi, cut):
            cand = cut | jnp.left_shift(jnp.int32(1), (seq.bit_length() - 1) - bi)
            cand_b = jnp.broadcast_to(cand, (tq, 128))
            c = count(lambda k, base: jnp.logical_and(k == thr_b, base + lane < cand_b))
            return jnp.where(c < need, cand, cut)
        cut = lax.fori_loop(0, seq.bit_length(), idx_body, jnp.zeros((tq, 1), I32))
        cut_ref[...] = jnp.broadcast_to(cut, (tq, 128))

    cut_b = jnp.where(thr_b == INT_MIN, qpos, cut_ref[...])

    def tile_copy(t):
        return pltpu.make_async_copy(stage_ref.at[t],
                                     mask_hbm.at[b, pl.ds(pl.multiple_of(i * tq, tq), tq), pl.ds(t * tk, tk)],
                                     sem.at[0])

    @pl.when(jnp.logical_or(b > 0, i > 0))
    def _():
        for t in range(nt):
            tile_copy(t).wait()

    def mask_tile(t, carry):
        for c in range(nch):
            k = keys_ref[t, :, chunk(c)]
            sel = jnp.logical_or(k > thr_b, jnp.logical_and(k == thr_b, t * tk + c * 128 + lane <= cut_b))
            stage_ref[t, :, chunk(c)] = jnp.where(sel, 0.0, NEG_BIG).astype(stage_ref.dtype)
        return carry

    lax.fori_loop(0, nkt, mask_tile, 0)

    def dead_tile(t, carry):
        stage_ref[t] = jnp.full((tq, tk), NEG_BIG, stage_ref.dtype)
        return carry

    lax.fori_loop(nkt, nt, dead_tile, 0)
    for t in range(nt):
        tile_copy(t).start()

    @pl.when(jnp.logical_and(b == pl.num_programs(0) - 1, i == pl.num_programs(1) - 1))
    def _():
        for t in range(nt):
            tile_copy(t).wait()


def _select(qi_rot, w_s, ki_rot, topk):
    bsz, n_pair, s, _ = qi_rot.shape
    tq, tk = DSA_TQ_SEL, DSA_TK
    return pl.pallas_call(
        functools.partial(_select_kernel, topk=topk, seq=s), name="dsa_select",
        grid=(bsz, s // tq),
        in_specs=[pl.BlockSpec((1, n_pair, tq, 128), lambda b, i: (b, 0, i, 0)),
                  pl.BlockSpec((1, tq, 128), lambda b, i: (b, i, 0)),
                  pl.BlockSpec((1, s, 256), lambda b, i: (b, 0, 0))],
        out_specs=pl.BlockSpec(memory_space=pl.ANY),
        out_shape=jax.ShapeDtypeStruct((bsz, s, s), BF16),
        scratch_shapes=[pltpu.VMEM((s // tk, tq, tk), I32),
                        pltpu.VMEM((s // tk, tq, tk), I16),
                        pltpu.VMEM((IDX_N_HEADS, tq, 128), F32),
                        pltpu.VMEM((tq, 128), I32),
                        pltpu.VMEM((s // tk, tq, tk), BF16),
                        pltpu.SemaphoreType.DMA((1,))],
        compiler_params=_cparams(("arbitrary", "arbitrary")),
    )(qi_rot, w_s, ki_rot)


def _attn_kernel(qb_ref, kt_ref, q_ref, k_ref, v_ref, mask_ref, o_ref, m_ref, acc_ref, bias_ref, s_ref, p_ref, al_ref):
    tq, tk, rc = DSA_TQ, DSA_TK, DSA_RC
    rep = ATT_N_HEADS // ATT_N_KV_HEADS
    nch = tk // 128
    step = pl.program_id(1)
    qb = qb_ref[step]
    kt = kt_ref[step]

    @pl.when(kt == 0)
    def _():
        m_ref[...] = jnp.full(m_ref.shape, NEG_BIG, F32)
        acc_ref[...] = jnp.zeros(acc_ref.shape, F32)

    bias_ref[...] = mask_ref[0].astype(F32)

    for g in range(ATT_N_KV_HEADS):
        qg = q_ref[0, g * rep:(g + 1) * rep].reshape(rep * tq, ATT_HEAD_DIM)
        kg = k_ref[0, :, g * 128:(g + 1) * 128]
        s_ref[...] = lax.dot_general(qg, kg, (((1,), (1,)), ((), ())), preferred_element_type=F32)

        def chunk(ci, carry):
            rows = pl.ds(pl.multiple_of(ci * rc, rc), rc)
            brows = pl.ds(pl.multiple_of((ci % (tq // rc)) * rc, rc), rc)
            sc = [s_ref[rows, c * 128:(c + 1) * 128] + bias_ref[brows, c * 128:(c + 1) * 128] for c in range(nch)]
            mx = sc[0]
            for c in range(1, nch):
                mx = jnp.maximum(mx, sc[c])
            m_old = m_ref[g, rows, :]
            m_new = jnp.maximum(m_old, jnp.max(mx, axis=-1, keepdims=True))
            al_ref[rows, :] = jnp.exp2(m_old - m_new)
            m_ref[g, rows, :] = m_new
            for c in range(nch):
                p_ref[rows, c * 128:(c + 1) * 128] = jnp.exp2(sc[c] - m_new).astype(p_ref.dtype)
            return carry

        lax.fori_loop(0, rep * tq // rc, chunk, 0, unroll=True)
        pv = jnp.dot(p_ref[...], v_ref[0, :, g * 256:(g + 1) * 256], preferred_element_type=F32)
        al = al_ref[...]
        acc_ref[g, :, 0:128] = al * acc_ref[g, :, 0:128] + pv[:, 0:128]
        acc_ref[g, :, 128:256] = al * acc_ref[g, :, 128:256] + pv[:, 128:256]

    @pl.when(kt == (qb * tq + tq - 1) // tk)
    def _():
        for g in range(ATT_N_KV_HEADS):
            o = acc_ref[g, :, 0:128] / acc_ref[g, :, 128:256]
            for r in range(rep):
                h = g * rep + r
                o_ref[0, :, h * 128:(h + 1) * 128] = o[r * tq:(r + 1) * tq].astype(o_ref.dtype)


def _attn(q_rot, k_rot, v_ext, mask):
    bsz, n_h, s, _ = q_rot.shape
    tq, tk = DSA_TQ, DSA_TK
    nkv = ATT_N_KV_HEADS * ATT_HEAD_DIM
    pairs = [(qb, kt) for qb in range(s // tq) for kt in range((qb * tq + tq - 1) // tk + 1)]
    qb_arr = jnp.asarray([p[0] for p in pairs], I32)
    kt_arr = jnp.asarray([p[1] for p in pairs], I32)
    rep = n_h // ATT_N_KV_HEADS
    grid_spec = pltpu.PrefetchScalarGridSpec(
        num_scalar_prefetch=2,
        grid=(bsz, len(pairs)),
        in_specs=[pl.BlockSpec((1, n_h, tq, 128), lambda b, t, qb, kt: (b, 0, qb[t], 0)),
                  pl.BlockSpec((1, tk, nkv), lambda b, t, qb, kt: (b, kt[t], 0)),
                  pl.BlockSpec((1, tk, 2 * nkv), lambda b, t, qb, kt: (b, kt[t], 0)),
                  pl.BlockSpec((1, tq, tk), lambda b, t, qb, kt: (b, qb[t], kt[t]))],
        out_specs=pl.BlockSpec((1, tq, n_h * 128), lambda b, t, qb, kt: (b, qb[t], 0)),
        scratch_shapes=[pltpu.VMEM((ATT_N_KV_HEADS, rep * tq, 128), F32),
                        pltpu.VMEM((ATT_N_KV_HEADS, rep * tq, 256), F32),
                        pltpu.VMEM((tq, tk), F32),
                        pltpu.VMEM((rep * tq, tk), F32),
                        pltpu.VMEM((rep * tq, tk), BF16),
                        pltpu.VMEM((rep * tq, 128), F32)])
    return pl.pallas_call(
        _attn_kernel, name="dsa_attn",
        grid_spec=grid_spec,
        out_shape=jax.ShapeDtypeStruct((bsz, s, n_h * 128), BF16),
        compiler_params=_cparams(("arbitrary", "arbitrary")),
    )(qb_arr, kt_arr, q_rot, k_rot, v_ext, mask)


def _merge_kernel(y_ref, o_ref, gs_ref, ga_ref, ws_ref, wa_ref, out_ref):
    a = jnp.dot(y_ref[0], ws_ref[...], preferred_element_type=F32)
    b = jnp.dot(o_ref[0], wa_ref[...], preferred_element_type=F32)
    out_ref[0] = (_sigmoid(gs_ref[0].astype(F32)) * a + _sigmoid(ga_ref[0].astype(F32)) * b).astype(out_ref.dtype)


def _merge(y_ssm, o_att, proj, w_ssm, w_att, tm, tn):
    bsz, s, d_inner = y_ssm.shape
    d_att = o_att.shape[2]
    d = w_ssm.shape[1]
    gs_off = 2 * d_inner + 2 * SSM_N_GROUPS * SSM_D_STATE + d_att
    ga_off = gs_off + d
    return pl.pallas_call(
        _merge_kernel, name="merge",
        grid=(bsz, s // tm, d // tn),
        in_specs=[pl.BlockSpec((1, tm, d_inner), lambda b, i, j: (b, i, 0)),
                  pl.BlockSpec((1, tm, d_att), lambda b, i, j: (b, i, 0)),
                  pl.BlockSpec((1, tm, tn), lambda b, i, j: (b, i, gs_off // tn + j)),
                  pl.BlockSpec((1, tm, tn), lambda b, i, j: (b, i, ga_off // tn + j)),
                  pl.BlockSpec((d_inner, tn), lambda b, i, j: (0, j)),
                  pl.BlockSpec((d_att, tn), lambda b, i, j: (0, j))],
        out_specs=pl.BlockSpec((1, tm, tn), lambda b, i, j: (b, i, j)),
        out_shape=jax.ShapeDtypeStruct((bsz, s, d), BF16),
        compiler_params=_cparams(("arbitrary", "arbitrary", "arbitrary")),
    )(y_ssm, o_att, proj, proj, w_ssm, w_att)


def _post_kernel(m_ref, wo_ref, x_ref, mod_ref, g_ref, rw_ref, rb_ref, x1_ref, h2_ref, gate_ref, idx_ref):
    mod = mod_ref[0]
    out = jnp.dot(m_ref[0], wo_ref[...], preferred_element_type=F32)
    x1 = x_ref[0] + mod[2:3] * out
    x1_ref[0] = x1
    h2 = _modnorm(x1, g_ref[...], mod[4:5], mod[3:4])
    h2_ref[0] = h2
    logit = _dot_hi(h2, rw_ref[...]) + rb_ref[...]
    lane = lax.broadcasted_iota(I32, logit.shape, 1)
    lane_f = lane.astype(F32)
    vals, idxs = [], []
    for _ in range(TOP_K):
        m = jnp.max(logit, axis=-1, keepdims=True)
        idx = jnp.min(jnp.where(logit == m, lane_f, 128.0), axis=-1, keepdims=True)
        vals.append(m)
        idxs.append(idx)
        logit = jnp.where(lane_f == idx, -3e38, logit)
    es = [jnp.exp(v - vals[0]) for v in vals]
    denom = es[0] + es[1] + es[2] + es[3]
    gate = jnp.zeros(logit.shape, F32)
    sel = jnp.zeros(logit.shape, F32)
    for k in range(TOP_K):
        gate = jnp.where(lane == k, es[k] / denom, gate)
        sel = jnp.where(lane == k, idxs[k], sel)
    gate_ref[0] = gate
    idx_ref[0] = sel.astype(I32)


def _post(merged, w_o, x, mod, g, router_w, router_b, tm):
    bsz, s, d = x.shape
    n_e = router_w.shape[1]
    rw = jnp.zeros((d, 128), F32).at[:, :n_e].set(router_w.astype(F32))
    rb = jnp.full((1, 128), NEG_BIG, F32).at[0, :n_e].set(router_b.astype(F32))
    row = lambda w: pl.BlockSpec((1, tm, w), lambda b, i: (b, i, 0))
    return pl.pallas_call(
        _post_kernel, name="post_router",
        grid=(bsz, s // tm),
        in_specs=[row(d),
                  pl.BlockSpec((d, d), lambda b, i: (0, 0)),
                  row(d),
                  pl.BlockSpec((1, 6, d), lambda b, i: (b, 0, 0)),
                  pl.BlockSpec((1, d), lambda b, i: (0, 0)),
                  pl.BlockSpec((d, 128), lambda b, i: (0, 0)),
                  pl.BlockSpec((1, 128), lambda b, i: (0, 0))],
        out_specs=[row(d), row(d), row(128), row(128)],
        out_shape=[jax.ShapeDtypeStruct((bsz, s, d), F32),
                   jax.ShapeDtypeStruct((bsz, s, d), F32),
                   jax.ShapeDtypeStruct((bsz, s, 128), F32),
                   jax.ShapeDtypeStruct((bsz, s, 128), I32)],
        compiler_params=_cparams(("arbitrary", "arbitrary")),
    )(merged, w_o, x, mod, g.reshape(1, d), rw, rb)


MOE_TM = 512
MOE_TN = 512


def _moe_kernel(be_ref, nu_ref, rt_ref, rtn_ref, rd_ref, h_hbm, wg_ref, wu_ref, bg_ref, bu_ref, wd_ref, bd_ref,
                y_hbm, xf_ref, xb_ref, acc_ref, yo_ref, sem):
    tm = MOE_TM
    blk = pl.program_id(0)
    j = pl.program_id(1)
    n_j = pl.num_programs(1)
    n_used = nu_ref[0]

    def hbm_row(ref, r):
        return ref.at[lax.shift_right_logical(r, 3), pl.ds(r & 7, 1)]

    def gather_start(tok_ref):
        def body(o, c):
            for u in range(8):
                pltpu.make_async_copy(hbm_row(h_hbm, tok_ref[0, 0, o * 8 + u]), xf_ref.at[o, pl.ds(u, 1)],
                                      sem.at[0]).start()
            return c
        lax.fori_loop(0, tm // 8, body, 0)

    def all_rows_in():
        return pltpu.make_async_copy(h_hbm.at[pl.ds(0, tm // 8)], xf_ref, sem.at[0])

    def all_rows_out():
        return pltpu.make_async_copy(yo_ref, y_hbm.at[pl.ds(0, tm // 8)], sem.at[1])

    @pl.when(blk < n_used)
    def _():
        @pl.when(j == 0)
        def _():
            @pl.when(blk == 0)
            def _():
                gather_start(rt_ref)
            all_rows_in().wait()
            xb_ref[...] = xf_ref[...].reshape(xb_ref.shape).astype(BF16)
            acc_ref[...] = jnp.zeros(acc_ref.shape, F32)

            @pl.when(blk + 1 < n_used)
            def _():
                gather_start(rtn_ref)

        x = xb_ref[...]
        gate = jnp.dot(x, wg_ref[0], preferred_element_type=F32) + bg_ref[0]
        up = jnp.dot(x, wu_ref[0], preferred_element_type=F32) + bu_ref[0]
        gate = jnp.minimum(gate, SWIGLU_LIMIT)
        up = jnp.clip(up, -SWIGLU_LIMIT, SWIGLU_LIMIT)
        act = (up + 1.0) * (gate * _sigmoid(SWIGLU_ALPHA * gate))
        acc_ref[...] += jnp.dot(act.astype(BF16), wd_ref[0], preferred_element_type=F32)

        @pl.when(j == n_j - 1)
        def _():
            @pl.when(blk > 0)
            def _():
                all_rows_out().wait()

            yo_ref[...] = (acc_ref[...] + bd_ref[0]).reshape(yo_ref.shape)

            def body(o, c):
                for u in range(8):
                    pltpu.make_async_copy(yo_ref.at[o, pl.ds(u, 1)], hbm_row(y_hbm, rd_ref[0, 0, o * 8 + u]),
                                          sem.at[1]).start()
                return c
            lax.fori_loop(0, tm // 8, body, 0)

            @pl.when(blk == n_used - 1)
            def _():
                all_rows_out().wait()

    @pl.when(jnp.logical_and(blk >= n_used, j == 0))
    def _():
        @pl.when(blk == n_used)
        def _():
            yo_ref[...] = jnp.zeros(yo_ref.shape, F32)
        fill = pltpu.make_async_copy(yo_ref, y_hbm.at[pl.ds(blk * (tm // 8), tm // 8)], sem.at[1])
        fill.start()
        fill.wait()


def _moe(h2, gate_idx, w_gu, b_gu, w_dn, b_dn):
    n_tok, d = h2.shape
    n_e, _, d2 = w_gu.shape
    d_e = d2 // 2
    tm, tn = MOE_TM, MOE_TN
    n_j = d_e // tn
    n_asg = n_tok * TOP_K
    e_flat = gate_idx[:, :TOP_K].reshape(-1)
    onehot = (e_flat[:, None] == jnp.arange(n_e, dtype=I32)[None, :]).astype(I32)
    csum = jnp.cumsum(onehot, axis=0)
    rank = jnp.take_along_axis(csum, e_flat[:, None], axis=1)[:, 0] - 1
    counts = csum[-1]
    padded = (counts + tm - 1) // tm * tm
    ends = jnp.cumsum(padded)
    dest = (ends - padded)[e_flat] + rank
    n_blocks = -(-(n_asg + n_e * (tm - 1)) // tm)
    row_asg = jnp.full((n_blocks * tm,), -1, I32).at[dest].set(jnp.arange(n_asg, dtype=I32))
    is_pad = row_asg < 0
    pad_rank = jnp.cumsum(is_pad.astype(I32)) - 1
    row_tok = jnp.where(is_pad, 0, row_asg // TOP_K).reshape(n_blocks, 1, tm)
    row_dst = jnp.where(is_pad, n_asg + pad_rank, (row_asg % TOP_K) * n_tok + row_asg // TOP_K)
    row_dst = row_dst.reshape(n_blocks, 1, tm)
    block_expert = jnp.minimum(
        jnp.searchsorted(ends, jnp.arange(n_blocks, dtype=I32) * tm, side='right'), n_e - 1).astype(I32)
    n_used = (ends[-1:] // tm).astype(I32)

    def jj(blk, j, nu):
        return jnp.where(blk < nu[0], j, n_j - 1)

    smem_rows = pl.BlockSpec((1, 1, tm), lambda blk, j, be, nu: (blk, 0, 0), memory_space=pltpu.SMEM)
    smem_next = pl.BlockSpec((1, 1, tm), lambda blk, j, be, nu: (jnp.minimum(blk + 1, n_blocks - 1), 0, 0),
                             memory_space=pltpu.SMEM)
    grid_spec = pltpu.PrefetchScalarGridSpec(
        num_scalar_prefetch=2,
        grid=(n_blocks, n_j),
        in_specs=[smem_rows, smem_next, smem_rows,
                  pl.BlockSpec(memory_space=pl.ANY),
                  pl.BlockSpec((1, d, tn), lambda blk, j, be, nu: (be[blk], 0, jj(blk, j, nu))),
                  pl.BlockSpec((1, d, tn), lambda blk, j, be, nu: (be[blk], 0, n_j + jj(blk, j, nu))),
                  pl.BlockSpec((1, 1, tn), lambda blk, j, be, nu: (be[blk], 0, jj(blk, j, nu))),
                  pl.BlockSpec((1, 1, tn), lambda blk, j, be, nu: (be[blk], 0, n_j + jj(blk, j, nu))),
                  pl.BlockSpec((1, tn, d), lambda blk, j, be, nu: (be[blk], jj(blk, j, nu), 0)),
                  pl.BlockSpec((1, 1, d), lambda blk, j, be, nu: (be[blk], 0, 0))],
        out_specs=pl.BlockSpec(memory_space=pl.ANY),
        scratch_shapes=[pltpu.VMEM((tm // 8, 8, d), F32),
                        pltpu.VMEM((tm, d), BF16),
                        pltpu.VMEM((tm, d), F32),
                        pltpu.VMEM((tm // 8, 8, d), F32),
                        pltpu.SemaphoreType.DMA((2,))])
    y = pl.pallas_call(
        _moe_kernel, name="moe_experts",
        grid_spec=grid_spec,
        out_shape=jax.ShapeDtypeStruct((n_blocks * tm // 8, 8, d), F32),
        compiler_params=_cparams(("arbitrary", "arbitrary")),
    )(block_expert, n_used, row_tok, row_tok, row_dst, h2.reshape(n_tok // 8, 8, d), w_gu, w_gu,
      b_gu.reshape(n_e, 1, d2), b_gu.reshape(n_e, 1, d2), w_dn, b_dn.reshape(n_e, 1, d))
    return y.reshape(n_blocks * tm, d)


def _combine_kernel(x1_ref, y0_ref, y1_ref, y2_ref, y3_ref, gate_ref, mod_ref, g_ref, o_ref, *, final):
    gate = gate_ref[0]
    moe = gate[:, 0:1] * y0_ref[...]
    for k, y_ref in ((1, y1_ref), (2, y2_ref), (3, y3_ref)):
        moe = moe + gate[:, k:k + 1] * y_ref[...]
    x2 = x1_ref[0] + mod_ref[0][5:6] * moe
    if final:
        ms = jnp.mean(x2 * x2, axis=-1, keepdims=True)
        x2 = x2 * lax.rsqrt(ms + NORM_EPS) * g_ref[...]
    o_ref[0] = x2


def _combine(x1, y, gates, mod, g, final, tm):
    bsz, s, d = x1.shape
    assert TOP_K == 4
    row = lambda w: pl.BlockSpec((1, tm, w), lambda b, i: (b, i, 0))
    nb = s // tm

    def slot(k):
        return pl.BlockSpec((tm, d), lambda b, i: (k * bsz * nb + b * nb + i, 0))

    return pl.pallas_call(
        functools.partial(_combine_kernel, final=final), name="combine",
        grid=(bsz, nb),
        in_specs=[row(d), slot(0), slot(1), slot(2), slot(3), row(128),
                  pl.BlockSpec((1, 6, d), lambda b, i: (b, 0, 0)),
                  pl.BlockSpec((1, d), lambda b, i: (0, 0))],
        out_specs=row(d),
        out_shape=jax.ShapeDtypeStruct((bsz, s, d), F32),
        compiler_params=_cparams(("arbitrary", "arbitrary")),
    )(x1, y, y, y, y, gates, mod, g.reshape(1, d))


def _pick(n, *cands):
    for c in cands:
        if n % c == 0:
            return c
    return n


def kernel(x, c, positions, ada_w, ada_b, norm_mix_g, w_in, conv_w, conv_b, dt_bias, a_log, d_skip, ssm_norm_g,
           w_out_ssm, w_out_att, w_o, norm_ffn_g, router_w, router_b, expert_w_gate_up, expert_b_gate_up,
           expert_w_down, expert_b_down, final_norm_g):
    bsz, s, d = x.shape
    depth = ada_w.shape[0]
    topk = min(IDX_TOPK_MAX, s // 4)
    for l in range(depth):
        mod = _adaln(c, ada_w[l], ada_b[l]).reshape(bsz, 6, d)
        w_big, w_small = _pack_w_in(w_in[l])
        tm = _pick(s, 1024, 512, 256)
        proj = _inproj(x, norm_mix_g[l], mod, w_big, BF16, tm, 1280)
        small = _inproj(x, norm_mix_g[l], mod, w_small, F32, tm, w_small.shape[1])
        y_ssm = _ssd(proj, small, conv_w[l], conv_b[l], dt_bias[l], a_log[l], d_skip[l], ssm_norm_g[l], 2 * d)
        q_rot, k_rot, v_ext, qi_rot, ki_rot, w_s = _rope(proj, small, positions, d, _pick(s, 512, 256))
        mask = _select(qi_rot, w_s, ki_rot, topk)
        o_att = _attn(q_rot, k_rot, v_ext, mask)
        merged = _merge(y_ssm, o_att, proj, w_out_ssm[l].astype(BF16), w_out_att[l].astype(BF16),
                        _pick(s, 512, 256), 1024)
        x1, h2, gates, gate_idx = _post(merged, w_o[l].astype(BF16), x, mod, norm_ffn_g[l], router_w[l],
                                        router_b[l], 256)
        y = _moe(h2.reshape(bsz * s, d), gate_idx.reshape(bsz * s, 128), expert_w_gate_up[l].astype(BF16),
                 expert_b_gate_up[l], expert_w_down[l].astype(BF16), expert_b_down[l])
        x = _combine(x1, y, gates, mod, final_norm_g, l == depth - 1, 256)
    return x
```

```python
import functools

import jax
import jax.numpy as jnp
import numpy as np
from jax import lax
from jax.experimental import pallas as pl
from jax.experimental.pallas import tpu as pltpu

F32 = jnp.float32
BF16 = jnp.bfloat16
I32 = jnp.int32
I16 = jnp.int16

SSM_HEAD_DIM = 64
SSM_N_GROUPS = 8
SSM_D_STATE = 128
SSM_CONV = 4
SSM_CHUNK = 128
ATT_N_HEADS = 16
ATT_N_KV_HEADS = 4
ATT_HEAD_DIM = 128
IDX_N_HEADS = 8
IDX_HEAD_DIM = 64
IDX_TOPK_MAX = 256
ROPE_THETA = 10000.0
N_EXPERTS = 32
TOP_K = 4
SWIGLU_ALPHA = 1.702
SWIGLU_LIMIT = 7.0
NORM_EPS = 1e-6

INT_MIN = -2147483648
NEG_BIG = -1e30
VMEM_LIMIT = 56 * 1024 * 1024


def _cparams(sem):
    return pltpu.CompilerParams(dimension_semantics=sem, vmem_limit_bytes=VMEM_LIMIT)


def _split3(a):
    hi = a.astype(BF16)
    r1 = a - hi.astype(F32)
    mid = r1.astype(BF16)
    lo = (r1 - mid.astype(F32)).astype(BF16)
    return hi, mid, lo


def _dot_x3(a, b_bf16, dims=None):
    out = None
    for p in _split3(a):
        if dims is None:
            t = jnp.dot(p, b_bf16, preferred_element_type=F32)
        else:
            t = lax.dot_general(p, b_bf16, dims, preferred_element_type=F32)
        out = t if out is None else out + t
    return out


def _dot_hi(a, b):
    a0, a1, a2 = _split3(a)
    b0, b1, b2 = _split3(b)
    d = functools.partial(jnp.dot, preferred_element_type=F32)
    return (d(a0, b0) + (d(a0, b1) + d(a1, b0))
            + (d(a0, b2) + d(a2, b0) + d(a1, b1)))


def _sigmoid(x):
    return 1.0 / (1.0 + jnp.exp(-x))


def _silu(x):
    return x * _sigmoid(x)


def _adaln_kernel(c_ref, w_ref, b_ref, o_ref):
    c = c_ref[...]
    o_ref[...] = _dot_hi(_silu(c), w_ref[...]) + b_ref[...]


def _adaln(c, ada_w, ada_b):
    bsz, d = c.shape
    n = ada_w.shape[1]
    rows = 8
    cp = jnp.zeros((rows, d), F32).at[:bsz].set(c)
    tn = 1024
    out = pl.pallas_call(
        _adaln_kernel, name="adaln",
        grid=(n // tn,),
        in_specs=[pl.BlockSpec((rows, d), lambda j: (0, 0)),
                  pl.BlockSpec((d, tn), lambda j: (0, j)),
                  pl.BlockSpec((1, tn), lambda j: (0, j))],
        out_specs=pl.BlockSpec((rows, tn), lambda j: (0, j)),
        out_shape=jax.ShapeDtypeStruct((rows, n), F32),
        compiler_params=_cparams(("arbitrary",)),
    )(cp, ada_w, ada_b.reshape(1, n))
    return out[:bsz]


def _modnorm(x, g, sc, sh):
    ms = jnp.mean(x * x, axis=-1, keepdims=True)
    y = x * lax.rsqrt(ms + NORM_EPS) * g
    return y * (1.0 + sc) + sh


def _inproj_kernel(x_ref, g_ref, mod_ref, w_ref, o_ref, h_ref, *, sc_row, sh_row):
    @pl.when(pl.program_id(2) == 0)
    def _():
        m = mod_ref[0]
        h = _modnorm(x_ref[0], g_ref[...], m[sc_row:sc_row + 1], m[sh_row:sh_row + 1])
        h_ref[...] = h.astype(BF16)

    o_ref[0] = jnp.dot(h_ref[...], w_ref[...], preferred_element_type=F32).astype(o_ref.dtype)


def _inproj(x, g, mod, w_bf16, out_dtype, tm, tn):
    bsz, s, d = x.shape
    n = w_bf16.shape[1]
    return pl.pallas_call(
        functools.partial(_inproj_kernel, sc_row=1, sh_row=0), name="inproj",
        grid=(bsz, s // tm, n // tn),
        in_specs=[pl.BlockSpec((1, tm, d), lambda b, i, j: (b, i, 0)),
                  pl.BlockSpec((1, d), lambda b, i, j: (0, 0)),
                  pl.BlockSpec((1, 6, d), lambda b, i, j: (b, 0, 0)),
                  pl.BlockSpec((d, tn), lambda b, i, j: (0, j))],
        out_specs=pl.BlockSpec((1, tm, tn), lambda b, i, j: (b, i, j)),
        out_shape=jax.ShapeDtypeStruct((bsz, s, n), out_dtype),
        scratch_shapes=[pltpu.VMEM((tm, d), BF16)],
        compiler_params=_cparams(("arbitrary", "arbitrary", "arbitrary")),
    )(x, g.reshape(1, d), mod, w_bf16)


def _ssd_kernel(z_ref, xs_ref, bm_ref, cm_ref, dt_ref, cw_ref, cb_ref, dtb_ref, alog_ref,
                dsk_ref, ng_ref, e_ref, o_ref, ubuf_ref, state_ref, y_ref):
    L = SSM_CHUNK
    G = SSM_N_GROUPS
    N = SSM_D_STATE
    P = SSM_HEAD_DIM
    d_inner = xs_ref.shape[2]
    H = d_inner // P
    R = H // G
    GW = R * P
    c_idx = pl.program_id(1)

    @pl.when(c_idx == 0)
    def _():
        ubuf_ref[0:8, :] = jnp.zeros((8, ubuf_ref.shape[1]), F32)
        state_ref[...] = jnp.zeros(state_ref.shape, F32)

    ubuf_ref[8:8 + L, 0:d_inner] = xs_ref[0].astype(F32)
    ubuf_ref[8:8 + L, d_inner:d_inner + G * N] = bm_ref[0].astype(F32)
    ubuf_ref[8:8 + L, d_inner + G * N:] = cm_ref[0].astype(F32)
    conv = cb_ref[...]
    for j in range(SSM_CONV):
        conv = conv + cw_ref[j:j + 1, :] * ubuf_ref[5 + j:5 + j + L, :]
    ubuf_ref[0:8, :] = ubuf_ref[L:L + 8, :]
    conv = _silu(conv)
    xs = conv[:, :d_inner]
    bmat = conv[:, d_inner:d_inner + G * N].astype(BF16)
    cmat = conv[:, d_inner + G * N:].astype(BF16)

    dtx = dt_ref[0][:, :H] + dtb_ref[...]
    dt = jnp.maximum(dtx, 0.0) + jnp.log(1.0 + jnp.exp(-jnp.abs(dtx)))
    a_neg = -jnp.exp(alog_ref[...])
    d_a = dt * a_neg

    row = lax.broadcasted_iota(I32, (L, L), 0)
    col = lax.broadcasted_iota(I32, (L, L), 1)
    causal = col <= row
    lmat = jnp.where(causal, 1.0, 0.0).astype(BF16)
    acum = None
    for p in _split3(d_a):
        t = jnp.dot(lmat, p, preferred_element_type=F32)
        acum = t if acum is None else acum + t
    acum_t = jnp.transpose(acum)

    e_mat = e_ref[...]
    dt_e = _dot_x3(dt, e_mat)
    acum_e = _dot_x3(acum, e_mat)
    exp_a_e = jnp.exp(acum_e)
    to_end_e = jnp.exp(acum_e[L - 1:L, :] - acum_e)

    xdt = xs * dt_e
    xdt_b = xdt.astype(BF16)
    xend_b = (xdt * to_end_e).astype(BF16)

    for g in range(G):
        b_g = bmat[:, g * N:(g + 1) * N]
        c_g = cmat[:, g * N:(g + 1) * N]
        cb = lax.dot_general(c_g, b_g, (((1,), (1,)), ((), ())), preferred_element_type=F32)
        st = state_ref[g]
        y_off = jnp.dot(c_g, st.astype(BF16), preferred_element_type=F32)
        y_ref[:, g * GW:(g + 1) * GW] = y_off * exp_a_e[:, g * GW:(g + 1) * GW]
        upd = lax.dot_general(b_g, xend_b[:, g * GW:(g + 1) * GW], (((0,), (0,)), ((), ())),
                              preferred_element_type=F32)
        state_ref[g] = st * exp_a_e[L - 1:L, g * GW:(g + 1) * GW] + upd
        for r in range(R):
            h = g * R + r
            seg = acum[:, h:h + 1] - acum_t[h:h + 1, :]
            decay = jnp.where(causal, jnp.exp(jnp.minimum(seg, 0.0)), 0.0)
            m = (cb * decay).astype(BF16)
            y_h = jnp.dot(m, xdt_b[:, h * P:(h + 1) * P], preferred_element_type=F32)
            y_ref[:, h * P:(h + 1) * P] += y_h

    y = y_ref[...] + xs * dsk_ref[...]
    yg = y * _silu(z_ref[0].astype(F32))
    for g in range(G):
        blk = yg[:, g * GW:(g + 1) * GW]
        ms = jnp.mean(blk * blk, axis=-1, keepdims=True)
        o_ref[0, :, g * GW:(g + 1) * GW] = (
            blk * lax.rsqrt(ms + NORM_EPS) * ng_ref[:, g * GW:(g + 1) * GW]).astype(o_ref.dtype)


def _ssd(proj, small, conv_w, conv_b, dt_bias, a_log, d_skip, norm_g, d_inner):
    bsz, s, _ = proj.shape
    L = SSM_CHUNK
    G, N, P = SSM_N_GROUPS, SSM_D_STATE, SSM_HEAD_DIM
    H = d_inner // P
    cdim = d_inner + 2 * G * N
    gw = d_inner // G
    zb = d_inner // d_inner
    bc0 = 2 * d_inner // (G * N)
    e_mat = (jnp.arange(d_inner, dtype=I32)[None, :] // P == jnp.arange(H, dtype=I32)[:, None]).astype(BF16)
    return pl.pallas_call(
        _ssd_kernel, name="ssd",
        grid=(bsz, s // L),
        in_specs=[pl.BlockSpec((1, L, d_inner), lambda b, c: (b, c, 0)),
                  pl.BlockSpec((1, L, d_inner), lambda b, c: (b, c, zb)),
                  pl.BlockSpec((1, L, G * N), lambda b, c: (b, c, bc0)),
                  pl.BlockSpec((1, L, G * N), lambda b, c: (b, c, bc0 + 1)),
                  pl.BlockSpec((1, L, 128), lambda b, c: (b, c, 2)),
                  pl.BlockSpec((SSM_CONV, cdim), lambda b, c: (0, 0)),
                  pl.BlockSpec((1, cdim), lambda b, c: (0, 0)),
                  pl.BlockSpec((1, H), lambda b, c: (0, 0)),
                  pl.BlockSpec((1, H), lambda b, c: (0, 0)),
                  pl.BlockSpec((1, d_inner), lambda b, c: (0, 0)),
                  pl.BlockSpec((1, d_inner), lambda b, c: (0, 0)),
                  pl.BlockSpec((H, d_inner), lambda b, c: (0, 0))],
        out_specs=pl.BlockSpec((1, L, d_inner), lambda b, c: (b, c, 0)),
        out_shape=jax.ShapeDtypeStruct((bsz, s, d_inner), BF16),
        scratch_shapes=[pltpu.VMEM((L + 8, cdim), F32),
                        pltpu.VMEM((G, N, gw), F32),
                        pltpu.VMEM((L, d_inner), F32)],
        compiler_params=_cparams(("arbitrary", "arbitrary")),
    )(proj, proj, proj, proj, small, conv_w, conv_b.reshape(1, cdim), dt_bias.reshape(1, H),
      a_log.reshape(1, H), jnp.repeat(d_skip, P).reshape(1, d_inner), norm_g.reshape(1, d_inner), e_mat)


def _pack_w_in(w_in):
    d = w_in.shape[0]
    d_inner = 2 * d
    gn = SSM_N_GROUPS * SSM_D_STATE
    n_h = d_inner // SSM_HEAD_DIM
    sizes = (d_inner, d_inner + 2 * gn, n_h, ATT_N_HEADS * ATT_HEAD_DIM, ATT_N_KV_HEADS * ATT_HEAD_DIM,
             ATT_N_KV_HEADS * ATT_HEAD_DIM, IDX_N_HEADS * IDX_HEAD_DIM, IDX_HEAD_DIM, IDX_N_HEADS, d, d)
    pts = [int(p) for p in np.cumsum(sizes)[:-1]]
    z, xbc, dt, q, k, v, qi, ki, wi, g_ssm, g_att = jnp.split(w_in, pts, axis=1)
    big = jnp.concatenate([z, xbc, q, g_ssm, g_att, k, v, qi], axis=1).astype(BF16)
    zeros = lambda n: jnp.zeros((d, n), w_in.dtype)
    small = jnp.concatenate([ki, zeros(128 - IDX_HEAD_DIM), wi, zeros(128 - IDX_N_HEADS), dt, zeros(128 - n_h)],
                            axis=1).astype(BF16)
    return big, small


def _rope_kernel(q_ref, k_ref, v_ref, qi_ref, sm_ref, wi_ref, ca_ref, sa_ref, ci_ref, si_ref,
                 qo_ref, ko_ref, vo_ref, qio_ref, kio_ref, wo_ref):
    ca = ca_ref[0]
    sa = sa_ref[0]
    ci = ci_ref[0]
    si = si_ref[0]
    att_scale = ATT_HEAD_DIM ** -0.5 * 1.4426950408889634
    q = q_ref[0].astype(F32)
    for h in range(ATT_N_HEADS):
        x = q[:, h * 128:(h + 1) * 128]
        qo_ref[0, h] = ((x * ca + pltpu.roll(x, 64, 1) * sa) * att_scale).astype(qo_ref.dtype)
    k = k_ref[0].astype(F32)
    for h in range(ATT_N_KV_HEADS):
        x = k[:, h * 128:(h + 1) * 128]
        ko_ref[0, :, h * 128:(h + 1) * 128] = (x * ca + pltpu.roll(x, 64, 1) * sa).astype(ko_ref.dtype)
    lane = lax.broadcasted_iota(I32, ci.shape, 1)
    first_half = (lane % IDX_HEAD_DIM) < (IDX_HEAD_DIM // 2)

    def rot_i(x):
        sw = jnp.where(first_half, pltpu.roll(x, 128 - IDX_HEAD_DIM // 2, 1), pltpu.roll(x, IDX_HEAD_DIM // 2, 1))
        return x * ci + sw * si

    qi = qi_ref[0].astype(F32)
    for j in range(IDX_N_HEADS * IDX_HEAD_DIM // 128):
        r = rot_i(qi[:, j * 128:(j + 1) * 128]).astype(qio_ref.dtype)
        for g in range(r.shape[0] // DSA_QG):
            qio_ref[0, g, j] = r[g * DSA_QG:(g + 1) * DSA_QG]
    ki = rot_i(sm_ref[0])
    kio_ref[0, :, 0:128] = ki.astype(kio_ref.dtype)
    kio_ref[0, :, 128:256] = pltpu.roll(ki, IDX_HEAD_DIM, 1).astype(kio_ref.dtype)
    wo_ref[0] = wi_ref[0] * (IDX_HEAD_DIM ** -0.5 * IDX_N_HEADS ** -0.5)
    v = v_ref[0]
    ones = jnp.ones((v.shape[0], 128), vo_ref.dtype)
    for h in range(ATT_N_KV_HEADS):
        vo_ref[0, :, h * 256:h * 256 + 128] = v[:, h * 128:(h + 1) * 128]
        vo_ref[0, :, h * 256 + 128:(h + 1) * 256] = ones


def _rope(proj, small, positions, d_model, tm):
    bsz, s, _ = proj.shape
    d_inner = 2 * d_model
    nq = ATT_N_HEADS * ATT_HEAD_DIM
    nkv = ATT_N_KV_HEADS * ATT_HEAD_DIM
    nqi = IDX_N_HEADS * IDX_HEAD_DIM
    q_off = 2 * d_inner + 2 * SSM_N_GROUPS * SSM_D_STATE
    k_off = q_off + nq + 2 * d_model
    qi_off = k_off + 2 * nkv

    def tables(dim):
        inv = ROPE_THETA ** (-jnp.arange(0, dim, 2, dtype=F32) / dim)
        ang = positions.astype(F32)[..., None] * inv
        return jnp.cos(ang), jnp.sin(ang)

    cos_a, sin_a = tables(ATT_HEAD_DIM)
    cos_i, sin_i = tables(IDX_HEAD_DIM)
    ca = jnp.concatenate([cos_a, cos_a], -1)
    sa = jnp.concatenate([-sin_a, sin_a], -1)
    ci = jnp.concatenate([cos_i] * 4, -1)
    si = jnp.concatenate([-sin_i, sin_i, -sin_i, sin_i], -1)
    tab = pl.BlockSpec((1, tm, 128), lambda b, i: (b, i, 0))
    return pl.pallas_call(
        _rope_kernel, name="rope",
        grid=(bsz, s // tm),
        in_specs=[pl.BlockSpec((1, tm, nq), lambda b, i: (b, i, q_off // nq)),
                  pl.BlockSpec((1, tm, nkv), lambda b, i: (b, i, k_off // nkv)),
                  pl.BlockSpec((1, tm, nkv), lambda b, i: (b, i, k_off // nkv + 1)),
                  pl.BlockSpec((1, tm, nqi), lambda b, i: (b, i, qi_off // nqi)),
                  pl.BlockSpec((1, tm, 128), lambda b, i: (b, i, 0)),
                  pl.BlockSpec((1, tm, 128), lambda b, i: (b, i, 1)),
                  tab, tab, tab, tab],
        out_specs=[pl.BlockSpec((1, ATT_N_HEADS, tm, 128), lambda b, i: (b, 0, i, 0)),
                   pl.BlockSpec((1, tm, nkv), lambda b, i: (b, i, 0)),
                   pl.BlockSpec((1, tm, 2 * nkv), lambda b, i: (b, i, 0)),
                   pl.BlockSpec((1, tm // DSA_QG, nqi // 128, DSA_QG, 128), lambda b, i: (b, i, 0, 0, 0)),
                   pl.BlockSpec((1, tm, 256), lambda b, i: (b, i, 0)),
                   tab],
        out_shape=[jax.ShapeDtypeStruct((bsz, ATT_N_HEADS, s, 128), BF16),
                   jax.ShapeDtypeStruct((bsz, s, nkv), BF16),
                   jax.ShapeDtypeStruct((bsz, s, 2 * nkv), BF16),
                   jax.ShapeDtypeStruct((bsz, s // DSA_QG, nqi // 128, DSA_QG, 128), BF16),
                   jax.ShapeDtypeStruct((bsz, s, 256), BF16),
                   jax.ShapeDtypeStruct((bsz, s, 128), F32)],
        compiler_params=_cparams(("arbitrary", "arbitrary")),
    )(proj, proj, proj, proj, small, small, ca, sa, ci, si)


DSA_TQ_SEL = 128
DSA_TQ = 256
DSA_TK = 512
DSA_RC = 64
DSA_QG = 16
DSA_FILL_TILES = 4


def _fill_head_weights(wb_ref, w):
    tq = w.shape[0]
    for h in range(IDX_N_HEADS):
        wb_ref[h] = jnp.broadcast_to(w[:, h:h + 1], (tq, 128))


def _score_keys(qi_groups, wb_ref, k2, emit):
    n_grp, n_pair, qg, _ = qi_groups.shape
    lhs = qi_groups.reshape(n_grp * n_pair * qg, 128)
    nt = (((1,), (1,)), ((), ()))
    l_even = lax.dot_general(lhs, k2[:, 0:128], nt, preferred_element_type=F32)
    l_odd = lax.dot_general(lhs, k2[:, 128:256], nt, preferred_element_type=F32)
    for g in range(n_grp):
        for c in range(k2.shape[0] // 128):
            sc = None
            for j in range(n_pair):
                r0 = (g * n_pair + j) * qg
                for l, h in ((l_even, 2 * j), (l_odd, 2 * j + 1)):
                    t = jnp.maximum(l[r0:r0 + qg, c * 128:(c + 1) * 128], 0.0) * wb_ref[h, g * qg:(g + 1) * qg, :]
                    sc = t if sc is None else sc + t
            bits = lax.bitcast_convert_type(sc, I32)
            bits = jnp.where(bits == INT_MIN, 0, bits)
            emit(g, c, bits ^ ((bits >> 31) & 0x7FFFFFFF))


def _select_kernel(qi_ref, w_ref, ki_ref, mask_hbm, keys_ref, h16_ref, wb_ref, cut_ref, stage_ref, sem,
                   *, topk, seq):
    tq, tk = DSA_TQ_SEL, DSA_TK
    nch = tk // 128
    nt = seq // tk
    b = pl.program_id(0)
    i = pl.program_id(1)
    nkt = (i * tq + tq - 1) // tk + 1
    qi = qi_ref[0]
    _fill_head_weights(wb_ref, w_ref[0])
    qpos = i * tq + lax.broadcasted_iota(I32, (tq, 128), 0)
    lane = lax.broadcasted_iota(I32, (tq, 128), 1)
    chunk = lambda c: slice(c * 128, (c + 1) * 128)

    tpi = DSA_FILL_TILES if nt % DSA_FILL_TILES == 0 else 1

    def fill(it, carry):
        for u in range(tpi):
            t = it * tpi + u

            def emit(g, c, key, t=t):
                rows = slice(g * DSA_QG, (g + 1) * DSA_QG)
                kpos = t * tk + c * 128 + lax.broadcasted_iota(I32, key.shape, 1)
                key = jnp.where(kpos <= i * tq + g * DSA_QG + lax.broadcasted_iota(I32, key.shape, 0), key, INT_MIN)
                keys_ref[t, rows, chunk(c)] = key
                h16_ref[t, rows, chunk(c)] = (key >> 16).astype(I16)
            _score_keys(qi, wb_ref, ki_ref[0, pl.ds(pl.multiple_of(t * tk, tk), tk), :], emit)
        return carry

    lax.fori_loop(0, (nkt + tpi - 1) // tpi, fill, 0)
    kf = float(topk)

    def search16():
        def count16(cand):
            def body(t, acc):
                for c in range(nch):
                    acc = acc + jnp.where(h16_ref[t, :, chunk(c)] >= cand, jnp.int16(1), jnp.int16(0))
                return acc
            acc = lax.fori_loop(0, nkt, body, jnp.zeros((tq, 128), I16))
            return jnp.sum(acc.astype(F32), axis=-1, keepdims=True)

        v = jnp.where(count16(jnp.zeros((tq, 128), I16)) >= kf, 0, -32768).astype(I32)

        def bit_body(bi, v):
            cand = v | jnp.left_shift(jnp.int32(1), 14 - bi)
            c = count16(jnp.broadcast_to(cand, (tq, 128)).astype(I16))
            return jnp.where(c >= kf, cand, v)
        return lax.fori_loop(0, 15, bit_body, v)

    thr_hi = search16()
    hi_b = jnp.broadcast_to(thr_hi, (tq, 128))

    def lower_half(t, carry):
        for c in range(nch):
            k = keys_ref[t, :, chunk(c)]
            hi = k >> 16
            lo = (k & 0xFFFF) - 32768
            h16_ref[t, :, chunk(c)] = jnp.where(hi == hi_b, lo, jnp.where(hi > hi_b, 32767, -32768)).astype(I16)
        return carry

    lax.fori_loop(0, nkt, lower_half, 0)
    thr = jnp.left_shift(thr_hi, 16) + (search16() + 32768)
    thr_b = jnp.broadcast_to(thr, (tq, 128))

    def count(pred):
        def body(t, acc):
            for c in range(nch):
                acc = acc + jnp.where(pred(keys_ref[t, :, chunk(c)], t * tk + c * 128), 1.0, 0.0)
            return acc
        acc = lax.fori_loop(0, nkt, body, jnp.zeros((tq, 128), F32))
        return jnp.sum(acc, axis=-1, keepdims=True)

    need = kf - count(lambda k, base: k > thr_b)
    n_eq = count(lambda k, base: k == thr_b)
    cut_ref[...] = jnp.full((tq, 128), seq, I32)
    partial = jnp.logical_and(n_eq > need, thr > INT_MIN)

    @pl.when(jnp.max(jnp.where(partial, 1.0, 0.0)) > 0.0)
    def _():
        def idx_body(bi, cut):
            cand = cut | jnp.left_shift(jnp.int32(1), (seq.bit_length() - 1) - bi)
            cand_b = jnp.broadcast_to(cand, (tq, 128))
            c = count(lambda k, base: jnp.logical_and(k == thr_b, base + lane < cand_b))
            return jnp.where(c < need, cand, cut)
        cut = lax.fori_loop(0, seq.bit_length(), idx_body, jnp.zeros((tq, 1), I32))
        cut_ref[...] = jnp.broadcast_to(cut, (tq, 128))

    cut_b = jnp.where(thr_b == INT_MIN, qpos, cut_ref[...])

    def tile_copy(t):
        return pltpu.make_async_copy(stage_ref.at[t], mask_hbm.at[b, i, t], sem.at[0])

    @pl.when(jnp.logical_or(b > 0, i > 0))
    def _():
        for t in range(nt):
            tile_copy(t).wait()

    def mask_tile(t, carry):
        for c in range(nch):
            k = keys_ref[t, :, chunk(c)]
            sel = jnp.logical_or(k > thr_b, jnp.logical_and(k == thr_b, t * tk + c * 128 + lane <= cut_b))
            stage_ref[t, :, chunk(c)] = jnp.where(sel, 0.0, NEG_BIG).astype(stage_ref.dtype)
        return carry

    lax.fori_loop(0, nkt, mask_tile, 0)

    def dead_tile(t, carry):
        stage_ref[t] = jnp.full((tq, tk), NEG_BIG, stage_ref.dtype)
        return carry

    lax.fori_loop(nkt, nt, dead_tile, 0)
    for t in range(nt):
        tile_copy(t).start()

    @pl.when(jnp.logical_and(b == pl.num_programs(0) - 1, i == pl.num_programs(1) - 1))
    def _():
        for t in range(nt):
            tile_copy(t).wait()


def _select(qi_rot, w_s, ki_rot, topk):
    bsz, n_grp, n_pair, qg, _ = qi_rot.shape
    s = n_grp * qg
    tq, tk = DSA_TQ_SEL, DSA_TK
    return pl.pallas_call(
        functools.partial(_select_kernel, topk=topk, seq=s), name="dsa_select",
        grid=(bsz, s // tq),
        in_specs=[pl.BlockSpec((1, tq // qg, n_pair, qg, 128), lambda b, i: (b, i, 0, 0, 0)),
                  pl.BlockSpec((1, tq, 128), lambda b, i: (b, i, 0)),
                  pl.BlockSpec((1, s, 256), lambda b, i: (b, 0, 0))],
        out_specs=pl.BlockSpec(memory_space=pl.ANY),
        out_shape=jax.ShapeDtypeStruct((bsz, s // tq, s // tk, tq, tk), BF16),
        scratch_shapes=[pltpu.VMEM((s // tk, tq, tk), I32),
                        pltpu.VMEM((s // tk, tq, tk), I16),
                        pltpu.VMEM((IDX_N_HEADS, tq, 128), F32),
                        pltpu.VMEM((tq, 128), I32),
                        pltpu.VMEM((s // tk, tq, tk), BF16),
                        pltpu.SemaphoreType.DMA((1,))],
        compiler_params=_cparams(("arbitrary", "arbitrary")),
    )(qi_rot, w_s, ki_rot)


def _attn_kernel(qb_ref, kt_ref, q_ref, k_ref, v_ref, mask_ref, o_ref, m_ref, acc_ref, bias_ref, s_ref, p_ref, al_ref):
    tq, tk, rc = DSA_TQ, DSA_TK, DSA_RC
    rep = ATT_N_HEADS // ATT_N_KV_HEADS
    nch = tk // 128
    step = pl.program_id(1)
    qb = qb_ref[step]
    kt = kt_ref[step]

    @pl.when(kt == 0)
    def _():
        m_ref[...] = jnp.full(m_ref.shape, NEG_BIG, F32)
        acc_ref[...] = jnp.zeros(acc_ref.shape, F32)

    bias_ref[...] = mask_ref[0, :, 0].reshape(tq, tk).astype(F32)

    for g in range(ATT_N_KV_HEADS):
        qg = q_ref[0, g * rep:(g + 1) * rep].reshape(rep * tq, ATT_HEAD_DIM)
        kg = k_ref[0, :, g * 128:(g + 1) * 128]
        s_ref[...] = lax.dot_general(qg, kg, (((1,), (1,)), ((), ())), preferred_element_type=F32)

        def chunk(ci, carry):
            rows = pl.ds(pl.multiple_of(ci * rc, rc), rc)
            brows = pl.ds(pl.multiple_of((ci % (tq // rc)) * rc, rc), rc)
            sc = [s_ref[rows, c * 128:(c + 1) * 128] + bias_ref[brows, c * 128:(c + 1) * 128] for c in range(nch)]
            mx = sc[0]
            for c in range(1, nch):
                mx = jnp.maximum(mx, sc[c])
            m_old = m_ref[g, rows, :]
            m_new = jnp.maximum(m_old, jnp.max(mx, axis=-1, keepdims=True))
            al_ref[rows, :] = jnp.exp2(m_old - m_new)
            m_ref[g, rows, :] = m_new
            for c in range(nch):
                p_ref[rows, c * 128:(c + 1) * 128] = jnp.exp2(sc[c] - m_new).astype(p_ref.dtype)
            return carry

        lax.fori_loop(0, rep * tq // rc, chunk, 0, unroll=True)
        pv = jnp.dot(p_ref[...], v_ref[0, :, g * 256:(g + 1) * 256], preferred_element_type=F32)
        al = al_ref[...]
        acc_ref[g, :, 0:128] = al * acc_ref[g, :, 0:128] + pv[:, 0:128]
        acc_ref[g, :, 128:256] = al * acc_ref[g, :, 128:256] + pv[:, 128:256]

    @pl.when(kt == (qb * tq + tq - 1) // tk)
    def _():
        for g in range(ATT_N_KV_HEADS):
            o = acc_ref[g, :, 0:128] / acc_ref[g, :, 128:256]
            for r in range(rep):
                h = g * rep + r
                o_ref[0, :, h * 128:(h + 1) * 128] = o[r * tq:(r + 1) * tq].astype(o_ref.dtype)


def _attn(q_rot, k_rot, v_ext, mask):
    bsz, n_h, s, _ = q_rot.shape
    tq, tk = DSA_TQ, DSA_TK
    nkv = ATT_N_KV_HEADS * ATT_HEAD_DIM
    pairs = [(qb, kt) for qb in range(s // tq) for kt in range((qb * tq + tq - 1) // tk + 1)]
    qb_arr = jnp.asarray([p[0] for p in pairs], I32)
    kt_arr = jnp.asarray([p[1] for p in pairs], I32)
    rep = n_h // ATT_N_KV_HEADS
    grid_spec = pltpu.PrefetchScalarGridSpec(
        num_scalar_prefetch=2,
        grid=(bsz, len(pairs)),
        in_specs=[pl.BlockSpec((1, n_h, tq, 128), lambda b, t, qb, kt: (b, 0, qb[t], 0)),
                  pl.BlockSpec((1, tk, nkv), lambda b, t, qb, kt: (b, kt[t], 0)),
                  pl.BlockSpec((1, tk, 2 * nkv), lambda b, t, qb, kt: (b, kt[t], 0)),
                  pl.BlockSpec((1, tq // DSA_TQ_SEL, 1, DSA_TQ_SEL, tk), lambda b, t, qb, kt: (b, qb[t], kt[t], 0, 0))],
        out_specs=pl.BlockSpec((1, tq, n_h * 128), lambda b, t, qb, kt: (b, qb[t], 0)),
        scratch_shapes=[pltpu.VMEM((ATT_N_KV_HEADS, rep * tq, 128), F32),
                        pltpu.VMEM((ATT_N_KV_HEADS, rep * tq, 256), F32),
                        pltpu.VMEM((tq, tk), F32),
                        pltpu.VMEM((rep * tq, tk), F32),
                        pltpu.VMEM((rep * tq, tk), BF16),
                        pltpu.VMEM((rep * tq, 128), F32)])
    return pl.pallas_call(
        _attn_kernel, name="dsa_attn",
        grid_spec=grid_spec,
        out_shape=jax.ShapeDtypeStruct((bsz, s, n_h * 128), BF16),
        compiler_params=_cparams(("arbitrary", "arbitrary")),
    )(qb_arr, kt_arr, q_rot, k_rot, v_ext, mask)


def _merge_kernel(y_ref, o_ref, gs_ref, ga_ref, ws_ref, wa_ref, out_ref):
    a = jnp.dot(y_ref[0], ws_ref[...], preferred_element_type=F32)
    b = jnp.dot(o_ref[0], wa_ref[...], preferred_element_type=F32)
    out_ref[0] = (_sigmoid(gs_ref[0].astype(F32)) * a + _sigmoid(ga_ref[0].astype(F32)) * b).astype(out_ref.dtype)


def _merge(y_ssm, o_att, proj, w_ssm, w_att, tm, tn):
    bsz, s, d_inner = y_ssm.shape
    d_att = o_att.shape[2]
    d = w_ssm.shape[1]
    gs_off = 2 * d_inner + 2 * SSM_N_GROUPS * SSM_D_STATE + d_att
    ga_off = gs_off + d
    return pl.pallas_call(
        _merge_kernel, name="merge",
        grid=(bsz, s // tm, d // tn),
        in_specs=[pl.BlockSpec((1, tm, d_inner), lambda b, i, j: (b, i, 0)),
                  pl.BlockSpec((1, tm, d_att), lambda b, i, j: (b, i, 0)),
                  pl.BlockSpec((1, tm, tn), lambda b, i, j: (b, i, gs_off // tn + j)),
                  pl.BlockSpec((1, tm, tn), lambda b, i, j: (b, i, ga_off // tn + j)),
                  pl.BlockSpec((d_inner, tn), lambda b, i, j: (0, j)),
                  pl.BlockSpec((d_att, tn), lambda b, i, j: (0, j))],
        out_specs=pl.BlockSpec((1, tm, tn), lambda b, i, j: (b, i, j)),
        out_shape=jax.ShapeDtypeStruct((bsz, s, d), BF16),
        compiler_params=_cparams(("arbitrary", "arbitrary", "arbitrary")),
    )(y_ssm, o_att, proj, proj, w_ssm, w_att)


def _post_kernel(m_ref, wo_ref, x_ref, mod_ref, g_ref, rw_ref, rb_ref, x1_ref, h2_ref, gate_ref, idx_ref):
    mod = mod_ref[0]
    out = jnp.dot(m_ref[0], wo_ref[...], preferred_element_type=F32)
    x1 = x_ref[0] + mod[2:3] * out
    x1_ref[0] = x1
    h2 = _modnorm(x1, g_ref[...], mod[4:5], mod[3:4])
    h2_ref[0] = h2
    logit = _dot_hi(h2, rw_ref[...]) + rb_ref[...]
    lane = lax.broadcasted_iota(I32, logit.shape, 1)
    lane_f = lane.astype(F32)
    vals, idxs = [], []
    for _ in range(TOP_K):
        m = jnp.max(logit, axis=-1, keepdims=True)
        idx = jnp.min(jnp.where(logit == m, lane_f, 128.0), axis=-1, keepdims=True)
        vals.append(m)
        idxs.append(idx)
        logit = jnp.where(lane_f == idx, -3e38, logit)
    es = [jnp.exp(v - vals[0]) for v in vals]
    denom = es[0] + es[1] + es[2] + es[3]
    gate = jnp.zeros(logit.shape, F32)
    sel = jnp.zeros(logit.shape, F32)
    for k in range(TOP_K):
        gate = jnp.where(lane == k, es[k] / denom, gate)
        sel = jnp.where(lane == k, idxs[k], sel)
    gate_ref[0] = gate
    idx_ref[0] = sel.astype(I32)


def _post(merged, w_o, x, mod, g, router_w, router_b, tm):
    bsz, s, d = x.shape
    n_e = router_w.shape[1]
    rw = jnp.zeros((d, 128), F32).at[:, :n_e].set(router_w.astype(F32))
    rb = jnp.full((1, 128), NEG_BIG, F32).at[0, :n_e].set(router_b.astype(F32))
    row = lambda w: pl.BlockSpec((1, tm, w), lambda b, i: (b, i, 0))
    return pl.pallas_call(
        _post_kernel, name="post_router",
        grid=(bsz, s // tm),
        in_specs=[row(d),
                  pl.BlockSpec((d, d), lambda b, i: (0, 0)),
                  row(d),
                  pl.BlockSpec((1, 6, d), lambda b, i: (b, 0, 0)),
                  pl.BlockSpec((1, d), lambda b, i: (0, 0)),
                  pl.BlockSpec((d, 128), lambda b, i: (0, 0)),
                  pl.BlockSpec((1, 128), lambda b, i: (0, 0))],
        out_specs=[row(d), row(d), row(128), row(128)],
        out_shape=[jax.ShapeDtypeStruct((bsz, s, d), F32),
                   jax.ShapeDtypeStruct((bsz, s, d), F32),
                   jax.ShapeDtypeStruct((bsz, s, 128), F32),
                   jax.ShapeDtypeStruct((bsz, s, 128), I32)],
        compiler_params=_cparams(("arbitrary", "arbitrary")),
    )(merged, w_o, x, mod, g.reshape(1, d), rw, rb)


MOE_TM = 512
MOE_TN = 512


def _moe_kernel(be_ref, nu_ref, rt_ref, rtn_ref, rd_ref, h_hbm, wg_ref, wu_ref, bg_ref, bu_ref, wd_ref, bd_ref,
                y_hbm, xf_ref, xb_ref, acc_ref, yo_ref, sem):
    tm = MOE_TM
    blk = pl.program_id(0)
    j = pl.program_id(1)
    n_j = pl.num_programs(1)
    n_used = nu_ref[0]

    def hbm_row(ref, r):
        return ref.at[lax.shift_right_logical(r, 3), pl.ds(r & 7, 1)]

    def gather_start(tok_ref):
        def body(o, c):
            for u in range(8):
                pltpu.make_async_copy(hbm_row(h_hbm, tok_ref[0, 0, o * 8 + u]), xf_ref.at[o, pl.ds(u, 1)],
                                      sem.at[0]).start()
            return c
        lax.fori_loop(0, tm // 8, body, 0)

    def all_rows_in():
        return pltpu.make_async_copy(h_hbm.at[pl.ds(0, tm // 8)], xf_ref, sem.at[0])

    def all_rows_out():
        return pltpu.make_async_copy(yo_ref, y_hbm.at[pl.ds(0, tm // 8)], sem.at[1])

    @pl.when(blk < n_used)
    def _():
        @pl.when(j == 0)
        def _():
            @pl.when(blk == 0)
            def _():
                gather_start(rt_ref)
            all_rows_in().wait()
            xb_ref[...] = xf_ref[...].reshape(xb_ref.shape).astype(BF16)
            acc_ref[...] = jnp.zeros(acc_ref.shape, F32)

            @pl.when(blk + 1 < n_used)
            def _():
                gather_start(rtn_ref)

        x = xb_ref[...]
        gate = jnp.dot(x, wg_ref[0], preferred_element_type=F32) + bg_ref[0]
        up = jnp.dot(x, wu_ref[0], preferred_element_type=F32) + bu_ref[0]
        gate = jnp.minimum(gate, SWIGLU_LIMIT)
        up = jnp.clip(up, -SWIGLU_LIMIT, SWIGLU_LIMIT)
        act = (up + 1.0) * (gate * _sigmoid(SWIGLU_ALPHA * gate))
        acc_ref[...] += jnp.dot(act.astype(BF16), wd_ref[0], preferred_element_type=F32)

        @pl.when(j == n_j - 1)
        def _():
            @pl.when(blk > 0)
            def _():
                all_rows_out().wait()

            yo_ref[...] = (acc_ref[...] + bd_ref[0]).reshape(yo_ref.shape)

            def body(o, c):
                for u in range(8):
                    pltpu.make_async_copy(yo_ref.at[o, pl.ds(u, 1)], hbm_row(y_hbm, rd_ref[0, 0, o * 8 + u]),
                                          sem.at[1]).start()
                return c
            lax.fori_loop(0, tm // 8, body, 0)

            @pl.when(blk == n_used - 1)
            def _():
                all_rows_out().wait()

    @pl.when(jnp.logical_and(blk >= n_used, j == 0))
    def _():
        @pl.when(blk == n_used)
        def _():
            yo_ref[...] = jnp.zeros(yo_ref.shape, F32)
        fill = pltpu.make_async_copy(yo_ref, y_hbm.at[pl.ds(blk * (tm // 8), tm // 8)], sem.at[1])
        fill.start()
        fill.wait()


def _moe(h2, gate_idx, w_gu, b_gu, w_dn, b_dn):
    n_tok, d = h2.shape
    n_e, _, d2 = w_gu.shape
    d_e = d2 // 2
    tm, tn = MOE_TM, MOE_TN
    n_j = d_e // tn
    n_asg = n_tok * TOP_K
    e_flat = gate_idx[:, :TOP_K].reshape(-1)
    onehot = (e_flat[:, None] == jnp.arange(n_e, dtype=I32)[None, :]).astype(I32)
    csum = jnp.cumsum(onehot, axis=0)
    rank = jnp.take_along_axis(csum, e_flat[:, None], axis=1)[:, 0] - 1
    counts = csum[-1]
    padded = (counts + tm - 1) // tm * tm
    ends = jnp.cumsum(padded)
    dest = (ends - padded)[e_flat] + rank
    n_blocks = -(-(n_asg + n_e * (tm - 1)) // tm)
    row_asg = jnp.full((n_blocks * tm,), -1, I32).at[dest].set(jnp.arange(n_asg, dtype=I32))
    is_pad = row_asg < 0
    pad_rank = jnp.cumsum(is_pad.astype(I32)) - 1
    row_tok = jnp.where(is_pad, 0, row_asg // TOP_K).reshape(n_blocks, 1, tm)
    row_dst = jnp.where(is_pad, n_asg + pad_rank, (row_asg % TOP_K) * n_tok + row_asg // TOP_K)
    row_dst = row_dst.reshape(n_blocks, 1, tm)
    block_expert = jnp.minimum(
        jnp.searchsorted(ends, jnp.arange(n_blocks, dtype=I32) * tm, side='right'), n_e - 1).astype(I32)
    n_used = (ends[-1:] // tm).astype(I32)

    def jj(blk, j, nu):
        return jnp.where(blk < nu[0], j, n_j - 1)

    smem_rows = pl.BlockSpec((1, 1, tm), lambda blk, j, be, nu: (blk, 0, 0), memory_space=pltpu.SMEM)
    smem_next = pl.BlockSpec((1, 1, tm), lambda blk, j, be, nu: (jnp.minimum(blk + 1, n_blocks - 1), 0, 0),
                             memory_space=pltpu.SMEM)
    grid_spec = pltpu.PrefetchScalarGridSpec(
        num_scalar_prefetch=2,
        grid=(n_blocks, n_j),
        in_specs=[smem_rows, smem_next, smem_rows,
                  pl.BlockSpec(memory_space=pl.ANY),
                  pl.BlockSpec((1, d, tn), lambda blk, j, be, nu: (be[blk], 0, jj(blk, j, nu))),
                  pl.BlockSpec((1, d, tn), lambda blk, j, be, nu: (be[blk], 0, n_j + jj(blk, j, nu))),
                  pl.BlockSpec((1, 1, tn), lambda blk, j, be, nu: (be[blk], 0, jj(blk, j, nu))),
                  pl.BlockSpec((1, 1, tn), lambda blk, j, be, nu: (be[blk], 0, n_j + jj(blk, j, nu))),
                  pl.BlockSpec((1, tn, d), lambda blk, j, be, nu: (be[blk], jj(blk, j, nu), 0)),
                  pl.BlockSpec((1, 1, d), lambda blk, j, be, nu: (be[blk], 0, 0))],
        out_specs=pl.BlockSpec(memory_space=pl.ANY),
        scratch_shapes=[pltpu.VMEM((tm // 8, 8, d), F32),
                        pltpu.VMEM((tm, d), BF16),
                        pltpu.VMEM((tm, d), F32),
                        pltpu.VMEM((tm // 8, 8, d), F32),
                        pltpu.SemaphoreType.DMA((2,))])
    y = pl.pallas_call(
        _moe_kernel, name="moe_experts",
        grid_spec=grid_spec,
        out_shape=jax.ShapeDtypeStruct((n_blocks * tm // 8, 8, d), F32),
        compiler_params=_cparams(("arbitrary", "arbitrary")),
    )(block_expert, n_used, row_tok, row_tok, row_dst, h2.reshape(n_tok // 8, 8, d), w_gu, w_gu,
      b_gu.reshape(n_e, 1, d2), b_gu.reshape(n_e, 1, d2), w_dn, b_dn.reshape(n_e, 1, d))
    return y.reshape(n_blocks * tm, d)


def _combine_kernel(x1_ref, y0_ref, y1_ref, y2_ref, y3_ref, gate_ref, mod_ref, g_ref, o_ref, *, final):
    gate = gate_ref[0]
    moe = gate[:, 0:1] * y0_ref[...]
    for k, y_ref in ((1, y1_ref), (2, y2_ref), (3, y3_ref)):
        moe = moe + gate[:, k:k + 1] * y_ref[...]
    x2 = x1_ref[0] + mod_ref[0][5:6] * moe
    if final:
        ms = jnp.mean(x2 * x2, axis=-1, keepdims=True)
        x2 = x2 * lax.rsqrt(ms + NORM_EPS) * g_ref[...]
    o_ref[0] = x2


def _combine(x1, y, gates, mod, g, final, tm):
    bsz, s, d = x1.shape
    assert TOP_K == 4
    row = lambda w: pl.BlockSpec((1, tm, w), lambda b, i: (b, i, 0))
    nb = s // tm

    def slot(k):
        return pl.BlockSpec((tm, d), lambda b, i: (k * bsz * nb + b * nb + i, 0))

    return pl.pallas_call(
        functools.partial(_combine_kernel, final=final), name="combine",
        grid=(bsz, nb),
        in_specs=[row(d), slot(0), slot(1), slot(2), slot(3), row(128),
                  pl.BlockSpec((1, 6, d), lambda b, i: (b, 0, 0)),
                  pl.BlockSpec((1, d), lambda b, i: (0, 0))],
        out_specs=row(d),
        out_shape=jax.ShapeDtypeStruct((bsz, s, d), F32),
        compiler_params=_cparams(("arbitrary", "arbitrary")),
    )(x1, y, y, y, y, gates, mod, g.reshape(1, d))


def _pick(n, *cands):
    for c in cands:
        if n % c == 0:
            return c
    return n


def kernel(x, c, positions, ada_w, ada_b, norm_mix_g, w_in, conv_w, conv_b, dt_bias, a_log, d_skip, ssm_norm_g,
           w_out_ssm, w_out_att, w_o, norm_ffn_g, router_w, router_b, expert_w_gate_up, expert_b_gate_up,
           expert_w_down, expert_b_down, final_norm_g):
    bsz, s, d = x.shape
    depth = ada_w.shape[0]
    topk = min(IDX_TOPK_MAX, s // 4)
    for l in range(depth):
        mod = _adaln(c, ada_w[l], ada_b[l]).reshape(bsz, 6, d)
        w_big, w_small = _pack_w_in(w_in[l])
        tm = _pick(s, 1024, 512, 256)
        proj = _inproj(x, norm_mix_g[l], mod, w_big, BF16, tm, 1280)
        small = _inproj(x, norm_mix_g[l], mod, w_small, F32, tm, w_small.shape[1])
        y_ssm = _ssd(proj, small, conv_w[l], conv_b[l], dt_bias[l], a_log[l], d_skip[l], ssm_norm_g[l], 2 * d)
        q_rot, k_rot, v_ext, qi_rot, ki_rot, w_s = _rope(proj, small, positions, d, _pick(s, 512, 256))
        mask = _select(qi_rot, w_s, ki_rot, topk)
        o_att = _attn(q_rot, k_rot, v_ext, mask)
        merged = _merge(y_ssm, o_att, proj, w_out_ssm[l].astype(BF16), w_out_att[l].astype(BF16),
                        _pick(s, 512, 256), 1024)
        x1, h2, gates, gate_idx = _post(merged, w_o[l].astype(BF16), x, mod, norm_ffn_g[l], router_w[l],
                                        router_b[l], 256)
        y = _moe(h2.reshape(bsz * s, d), gate_idx.reshape(bsz * s, 128), expert_w_gate_up[l].astype(BF16),
                 expert_b_gate_up[l], expert_w_down[l].astype(BF16), expert_b_down[l])
        x = _combine(x1, y, gates, mod, final_norm_g, l == depth - 1, 256)
    return x
```

```python
import functools

import jax
import jax.numpy as jnp
import numpy as np
from jax import lax
from jax.experimental import pallas as pl
from jax.experimental.pallas import tpu as pltpu

F32 = jnp.float32
BF16 = jnp.bfloat16
I32 = jnp.int32
I16 = jnp.int16

SSM_HEAD_DIM = 64
SSM_N_GROUPS = 8
SSM_D_STATE = 128
SSM_CONV = 4
SSM_CHUNK = 128
ATT_N_HEADS = 16
ATT_N_KV_HEADS = 4
ATT_HEAD_DIM = 128
IDX_N_HEADS = 8
IDX_HEAD_DIM = 64
IDX_TOPK_MAX = 256
ROPE_THETA = 10000.0
N_EXPERTS = 32
TOP_K = 4
SWIGLU_ALPHA = 1.702
SWIGLU_LIMIT = 7.0
NORM_EPS = 1e-6

INT_MIN = -2147483648
NEG_BIG = -1e30
VMEM_LIMIT = 56 * 1024 * 1024


def _cparams(sem):
    return pltpu.CompilerParams(dimension_semantics=sem, vmem_limit_bytes=VMEM_LIMIT)


def _split3(a):
    hi = a.astype(BF16)
    r1 = a - hi.astype(F32)
    mid = r1.astype(BF16)
    lo = (r1 - mid.astype(F32)).astype(BF16)
    return hi, mid, lo


def _dot_x3(a, b_bf16, dims=None):
    out = None
    for p in _split3(a):
        if dims is None:
            t = jnp.dot(p, b_bf16, preferred_element_type=F32)
        else:
            t = lax.dot_general(p, b_bf16, dims, preferred_element_type=F32)
        out = t if out is None else out + t
    return out


def _dot_hi(a, b):
    a0, a1, a2 = _split3(a)
    b0, b1, b2 = _split3(b)
    d = functools.partial(jnp.dot, preferred_element_type=F32)
    return (d(a0, b0) + (d(a0, b1) + d(a1, b0))
            + (d(a0, b2) + d(a2, b0) + d(a1, b1)))


def _sigmoid(x):
    return 1.0 / (1.0 + jnp.exp(-x))


def _silu(x):
    return x * _sigmoid(x)


def _adaln_kernel(c_ref, w_ref, b_ref, o_ref):
    c = c_ref[...]
    o_ref[...] = _dot_hi(_silu(c), w_ref[...]) + b_ref[...]


def _adaln(c, ada_w, ada_b):
    bsz, d = c.shape
    n = ada_w.shape[1]
    rows = 8
    cp = jnp.zeros((rows, d), F32).at[:bsz].set(c)
    tn = 1024
    out = pl.pallas_call(
        _adaln_kernel, name="adaln",
        grid=(n // tn,),
        in_specs=[pl.BlockSpec((rows, d), lambda j: (0, 0)),
                  pl.BlockSpec((d, tn), lambda j: (0, j)),
                  pl.BlockSpec((1, tn), lambda j: (0, j))],
        out_specs=pl.BlockSpec((rows, tn), lambda j: (0, j)),
        out_shape=jax.ShapeDtypeStruct((rows, n), F32),
        compiler_params=_cparams(("arbitrary",)),
    )(cp, ada_w, ada_b.reshape(1, n))
    return out[:bsz]


def _modnorm(x, g, sc, sh):
    ms = jnp.mean(x * x, axis=-1, keepdims=True)
    y = x * lax.rsqrt(ms + NORM_EPS) * g
    return y * (1.0 + sc) + sh


def _inproj_kernel(x_ref, g_ref, mod_ref, w_ref, o_ref, h_ref, *, sc_row, sh_row):
    @pl.when(pl.program_id(2) == 0)
    def _():
        m = mod_ref[0]
        h = _modnorm(x_ref[0], g_ref[...], m[sc_row:sc_row + 1], m[sh_row:sh_row + 1])
        h_ref[...] = h.astype(BF16)

    o_ref[0] = jnp.dot(h_ref[...], w_ref[...], preferred_element_type=F32).astype(o_ref.dtype)


def _inproj(x, g, mod, w_bf16, out_dtype, tm, tn):
    bsz, s, d = x.shape
    n = w_bf16.shape[1]
    return pl.pallas_call(
        functools.partial(_inproj_kernel, sc_row=1, sh_row=0), name="inproj",
        grid=(bsz, s // tm, n // tn),
        in_specs=[pl.BlockSpec((1, tm, d), lambda b, i, j: (b, i, 0)),
                  pl.BlockSpec((1, d), lambda b, i, j: (0, 0)),
                  pl.BlockSpec((1, 6, d), lambda b, i, j: (b, 0, 0)),
                  pl.BlockSpec((d, tn), lambda b, i, j: (0, j))],
        out_specs=pl.BlockSpec((1, tm, tn), lambda b, i, j: (b, i, j)),
        out_shape=jax.ShapeDtypeStruct((bsz, s, n), out_dtype),
        scratch_shapes=[pltpu.VMEM((tm, d), BF16)],
        compiler_params=_cparams(("arbitrary", "arbitrary", "arbitrary")),
    )(x, g.reshape(1, d), mod, w_bf16)


def _ssd_kernel(z_ref, xs_ref, bm_ref, cm_ref, dt_ref, cw_ref, cb_ref, dtb_ref, alog_ref,
                dsk_ref, ng_ref, e_ref, o_ref, ubuf_ref, state_ref, y_ref):
    L = SSM_CHUNK
    G = SSM_N_GROUPS
    N = SSM_D_STATE
    P = SSM_HEAD_DIM
    d_inner = xs_ref.shape[2]
    H = d_inner // P
    R = H // G
    GW = R * P
    c_idx = pl.program_id(1)

    @pl.when(c_idx == 0)
    def _():
        ubuf_ref[0:8, :] = jnp.zeros((8, ubuf_ref.shape[1]), F32)
        state_ref[...] = jnp.zeros(state_ref.shape, F32)

    ubuf_ref[8:8 + L, 0:d_inner] = xs_ref[0].astype(F32)
    ubuf_ref[8:8 + L, d_inner:d_inner + G * N] = bm_ref[0].astype(F32)
    ubuf_ref[8:8 + L, d_inner + G * N:] = cm_ref[0].astype(F32)
    conv = cb_ref[...]
    for j in range(SSM_CONV):
        conv = conv + cw_ref[j:j + 1, :] * ubuf_ref[5 + j:5 + j + L, :]
    ubuf_ref[0:8, :] = ubuf_ref[L:L + 8, :]
    conv = _silu(conv)
    xs = conv[:, :d_inner]
    bmat = conv[:, d_inner:d_inner + G * N].astype(BF16)
    cmat = conv[:, d_inner + G * N:].astype(BF16)

    dtx = dt_ref[0][:, :H] + dtb_ref[...]
    dt = jnp.maximum(dtx, 0.0) + jnp.log(1.0 + jnp.exp(-jnp.abs(dtx)))
    a_neg = -jnp.exp(alog_ref[...])
    d_a = dt * a_neg

    row = lax.broadcasted_iota(I32, (L, L), 0)
    col = lax.broadcasted_iota(I32, (L, L), 1)
    causal = col <= row
    lmat = jnp.where(causal, 1.0, 0.0).astype(BF16)
    acum = None
    for p in _split3(d_a):
        t = jnp.dot(lmat, p, preferred_element_type=F32)
        acum = t if acum is None else acum + t
    acum_t = jnp.transpose(acum)

    e_mat = e_ref[...]
    dt_e = _dot_x3(dt, e_mat)
    acum_e = _dot_x3(acum, e_mat)
    exp_a_e = jnp.exp(acum_e)
    to_end_e = jnp.exp(acum_e[L - 1:L, :] - acum_e)

    xdt = xs * dt_e
    xdt_b = xdt.astype(BF16)
    xend_b = (xdt * to_end_e).astype(BF16)

    for g in range(G):
        b_g = bmat[:, g * N:(g + 1) * N]
        c_g = cmat[:, g * N:(g + 1) * N]
        cb = lax.dot_general(c_g, b_g, (((1,), (1,)), ((), ())), preferred_element_type=F32)
        st = state_ref[g]
        y_off = jnp.dot(c_g, st.astype(BF16), preferred_element_type=F32)
        y_ref[:, g * GW:(g + 1) * GW] = y_off * exp_a_e[:, g * GW:(g + 1) * GW]
        upd = lax.dot_general(b_g, xend_b[:, g * GW:(g + 1) * GW], (((0,), (0,)), ((), ())),
                              preferred_element_type=F32)
        state_ref[g] = st * exp_a_e[L - 1:L, g * GW:(g + 1) * GW] + upd
        for r in range(R):
            h = g * R + r
            seg = acum[:, h:h + 1] - acum_t[h:h + 1, :]
            decay = jnp.where(causal, jnp.exp(jnp.minimum(seg, 0.0)), 0.0)
            m = (cb * decay).astype(BF16)
            y_h = jnp.dot(m, xdt_b[:, h * P:(h + 1) * P], preferred_element_type=F32)
            y_ref[:, h * P:(h + 1) * P] += y_h

    y = y_ref[...] + xs * dsk_ref[...]
    yg = y * _silu(z_ref[0].astype(F32))
    for g in range(G):
        blk = yg[:, g * GW:(g + 1) * GW]
        ms = jnp.mean(blk * blk, axis=-1, keepdims=True)
        o_ref[0, :, g * GW:(g + 1) * GW] = (
            blk * lax.rsqrt(ms + NORM_EPS) * ng_ref[:, g * GW:(g + 1) * GW]).astype(o_ref.dtype)


def _ssd(proj, small, conv_w, conv_b, dt_bias, a_log, d_skip, norm_g, d_inner):
    bsz, s, _ = proj.shape
    L = SSM_CHUNK
    G, N, P = SSM_N_GROUPS, SSM_D_STATE, SSM_HEAD_DIM
    H = d_inner // P
    cdim = d_inner + 2 * G * N
    gw = d_inner // G
    zb = d_inner // d_inner
    bc0 = 2 * d_inner // (G * N)
    e_mat = (jnp.arange(d_inner, dtype=I32)[None, :] // P == jnp.arange(H, dtype=I32)[:, None]).astype(BF16)
    return pl.pallas_call(
        _ssd_kernel, name="ssd",
        grid=(bsz, s // L),
        in_specs=[pl.BlockSpec((1, L, d_inner), lambda b, c: (b, c, 0)),
                  pl.BlockSpec((1, L, d_inner), lambda b, c: (b, c, zb)),
                  pl.BlockSpec((1, L, G * N), lambda b, c: (b, c, bc0)),
                  pl.BlockSpec((1, L, G * N), lambda b, c: (b, c, bc0 + 1)),
                  pl.BlockSpec((1, L, 128), lambda b, c: (b, c, 2)),
                  pl.BlockSpec((SSM_CONV, cdim), lambda b, c: (0, 0)),
                  pl.BlockSpec((1, cdim), lambda b, c: (0, 0)),
                  pl.BlockSpec((1, H), lambda b, c: (0, 0)),
                  pl.BlockSpec((1, H), lambda b, c: (0, 0)),
                  pl.BlockSpec((1, d_inner), lambda b, c: (0, 0)),
                  pl.BlockSpec((1, d_inner), lambda b, c: (0, 0)),
                  pl.BlockSpec((H, d_inner), lambda b, c: (0, 0))],
        out_specs=pl.BlockSpec((1, L, d_inner), lambda b, c: (b, c, 0)),
        out_shape=jax.ShapeDtypeStruct((bsz, s, d_inner), BF16),
        scratch_shapes=[pltpu.VMEM((L + 8, cdim), F32),
                        pltpu.VMEM((G, N, gw), F32),
                        pltpu.VMEM((L, d_inner), F32)],
        compiler_params=_cparams(("arbitrary", "arbitrary")),
    )(proj, proj, proj, proj, small, conv_w, conv_b.reshape(1, cdim), dt_bias.reshape(1, H),
      a_log.reshape(1, H), jnp.repeat(d_skip, P).reshape(1, d_inner), norm_g.reshape(1, d_inner), e_mat)


def _pack_w_in(w_in):
    d = w_in.shape[0]
    d_inner = 2 * d
    gn = SSM_N_GROUPS * SSM_D_STATE
    n_h = d_inner // SSM_HEAD_DIM
    sizes = (d_inner, d_inner + 2 * gn, n_h, ATT_N_HEADS * ATT_HEAD_DIM, ATT_N_KV_HEADS * ATT_HEAD_DIM,
             ATT_N_KV_HEADS * ATT_HEAD_DIM, IDX_N_HEADS * IDX_HEAD_DIM, IDX_HEAD_DIM, IDX_N_HEADS, d, d)
    pts = [int(p) for p in np.cumsum(sizes)[:-1]]
    z, xbc, dt, q, k, v, qi, ki, wi, g_ssm, g_att = jnp.split(w_in, pts, axis=1)
    big = jnp.concatenate([z, xbc, q, g_ssm, g_att, k, v, qi], axis=1).astype(BF16)
    zeros = lambda n: jnp.zeros((d, n), w_in.dtype)
    small = jnp.concatenate([ki, zeros(128 - IDX_HEAD_DIM), wi, zeros(128 - IDX_N_HEADS), dt, zeros(128 - n_h)],
                            axis=1).astype(BF16)
    return big, small


def _rope_kernel(q_ref, k_ref, v_ref, qi_ref, sm_ref, wi_ref, ca_ref, sa_ref, ci_ref, si_ref,
                 qo_ref, ko_ref, vo_ref, qio_ref, kio_ref, wo_ref):
    ca = ca_ref[0]
    sa = sa_ref[0]
    ci = ci_ref[0]
    si = si_ref[0]
    att_scale = ATT_HEAD_DIM ** -0.5 * 1.4426950408889634
    q = q_ref[0].astype(F32)
    for h in range(ATT_N_HEADS):
        x = q[:, h * 128:(h + 1) * 128]
        qo_ref[0, h] = ((x * ca + pltpu.roll(x, 64, 1) * sa) * att_scale).astype(qo_ref.dtype)
    k = k_ref[0].astype(F32)
    for h in range(ATT_N_KV_HEADS):
        x = k[:, h * 128:(h + 1) * 128]
        ko_ref[0, :, h * 128:(h + 1) * 128] = (x * ca + pltpu.roll(x, 64, 1) * sa).astype(ko_ref.dtype)
    lane = lax.broadcasted_iota(I32, ci.shape, 1)
    first_half = (lane % IDX_HEAD_DIM) < (IDX_HEAD_DIM // 2)

    def rot_i(x):
        sw = jnp.where(first_half, pltpu.roll(x, 128 - IDX_HEAD_DIM // 2, 1), pltpu.roll(x, IDX_HEAD_DIM // 2, 1))
        return x * ci + sw * si

    qi = qi_ref[0].astype(F32)
    for j in range(IDX_N_HEADS * IDX_HEAD_DIM // 128):
        r = rot_i(qi[:, j * 128:(j + 1) * 128]).astype(qio_ref.dtype)
        for g in range(r.shape[0] // DSA_QG):
            qio_ref[0, g, j] = r[g * DSA_QG:(g + 1) * DSA_QG]
    ki = rot_i(sm_ref[0])
    kio_ref[0, :, 0:128] = ki.astype(kio_ref.dtype)
    kio_ref[0, :, 128:256] = pltpu.roll(ki, IDX_HEAD_DIM, 1).astype(kio_ref.dtype)
    wo_ref[0] = wi_ref[0] * (IDX_HEAD_DIM ** -0.5 * IDX_N_HEADS ** -0.5)
    v = v_ref[0]
    ones = jnp.ones((v.shape[0], 128), vo_ref.dtype)
    for h in range(ATT_N_KV_HEADS):
        vo_ref[0, :, h * 256:h * 256 + 128] = v[:, h * 128:(h + 1) * 128]
        vo_ref[0, :, h * 256 + 128:(h + 1) * 256] = ones


def _rope(proj, small, positions, d_model, tm):
    bsz, s, _ = proj.shape
    d_inner = 2 * d_model
    nq = ATT_N_HEADS * ATT_HEAD_DIM
    nkv = ATT_N_KV_HEADS * ATT_HEAD_DIM
    nqi = IDX_N_HEADS * IDX_HEAD_DIM
    q_off = 2 * d_inner + 2 * SSM_N_GROUPS * SSM_D_STATE
    k_off = q_off + nq + 2 * d_model
    qi_off = k_off + 2 * nkv

    def tables(dim):
        inv = ROPE_THETA ** (-jnp.arange(0, dim, 2, dtype=F32) / dim)
        ang = positions.astype(F32)[..., None] * inv
        return jnp.cos(ang), jnp.sin(ang)

    cos_a, sin_a = tables(ATT_HEAD_DIM)
    cos_i, sin_i = tables(IDX_HEAD_DIM)
    ca = jnp.concatenate([cos_a, cos_a], -1)
    sa = jnp.concatenate([-sin_a, sin_a], -1)
    ci = jnp.concatenate([cos_i] * 4, -1)
    si = jnp.concatenate([-sin_i, sin_i, -sin_i, sin_i], -1)
    tab = pl.BlockSpec((1, tm, 128), lambda b, i: (b, i, 0))
    return pl.pallas_call(
        _rope_kernel, name="rope",
        grid=(bsz, s // tm),
        in_specs=[pl.BlockSpec((1, tm, nq), lambda b, i: (b, i, q_off // nq)),
                  pl.BlockSpec((1, tm, nkv), lambda b, i: (b, i, k_off // nkv)),
                  pl.BlockSpec((1, tm, nkv), lambda b, i: (b, i, k_off // nkv + 1)),
                  pl.BlockSpec((1, tm, nqi), lambda b, i: (b, i, qi_off // nqi)),
                  pl.BlockSpec((1, tm, 128), lambda b, i: (b, i, 0)),
                  pl.BlockSpec((1, tm, 128), lambda b, i: (b, i, 1)),
                  tab, tab, tab, tab],
        out_specs=[pl.BlockSpec((1, ATT_N_HEADS, tm, 128), lambda b, i: (b, 0, i, 0)),
                   pl.BlockSpec((1, tm, nkv), lambda b, i: (b, i, 0)),
                   pl.BlockSpec((1, tm, 2 * nkv), lambda b, i: (b, i, 0)),
                   pl.BlockSpec((1, tm // DSA_QG, nqi // 128, DSA_QG, 128), lambda b, i: (b, i, 0, 0, 0)),
                   pl.BlockSpec((1, tm, 256), lambda b, i: (b, i, 0)),
                   tab],
        out_shape=[jax.ShapeDtypeStruct((bsz, ATT_N_HEADS, s, 128), BF16),
                   jax.ShapeDtypeStruct((bsz, s, nkv), BF16),
                   jax.ShapeDtypeStruct((bsz, s, 2 * nkv), BF16),
                   jax.ShapeDtypeStruct((bsz, s // DSA_QG, nqi // 128, DSA_QG, 128), BF16),
                   jax.ShapeDtypeStruct((bsz, s, 256), BF16),
                   jax.ShapeDtypeStruct((bsz, s, 128), F32)],
        compiler_params=_cparams(("arbitrary", "arbitrary")),
    )(proj, proj, proj, proj, small, small, ca, sa, ci, si)


DSA_TQ_SEL = 128
DSA_TQ = 256
DSA_TK = 512
DSA_TK_ATT = 512
DSA_RC = 64
DSA_PV_SPLIT = 4
DSA_QG = 16
DSA_FILL_TILES = 4


def _fill_head_weights(wb_ref, w):
    tq = w.shape[0]
    for h in range(IDX_N_HEADS):
        wb_ref[h] = jnp.broadcast_to(w[:, h:h + 1], (tq, 128))


def _score_keys(qi_groups, wb_ref, k2, emit):
    n_grp, n_pair, qg, _ = qi_groups.shape
    lhs = qi_groups.reshape(n_grp * n_pair * qg, 128)
    nt = (((1,), (1,)), ((), ()))
    l_even = lax.dot_general(lhs, k2[:, 0:128], nt, preferred_element_type=F32)
    l_odd = lax.dot_general(lhs, k2[:, 128:256], nt, preferred_element_type=F32)
    for g in range(n_grp):
        for c in range(k2.shape[0] // 128):
            sc = None
            for j in range(n_pair):
                r0 = (g * n_pair + j) * qg
                for l, h in ((l_even, 2 * j), (l_odd, 2 * j + 1)):
                    t = jnp.maximum(l[r0:r0 + qg, c * 128:(c + 1) * 128], 0.0) * wb_ref[h, g * qg:(g + 1) * qg, :]
                    sc = t if sc is None else sc + t
            bits = lax.bitcast_convert_type(sc, I32)
            bits = jnp.where(bits == INT_MIN, 0, bits)
            emit(g, c, bits ^ ((bits >> 31) & 0x7FFFFFFF))


def _select_kernel(qi_ref, w_ref, ki_ref, mask_hbm, keys_ref, h16_ref, wb_ref, cut_ref, stage_ref, sem,
                   *, topk, seq):
    tq, tk = DSA_TQ_SEL, DSA_TK
    nch = tk // 128
    nt = seq // tk
    b = pl.program_id(0)
    i = pl.program_id(1)
    nkt = (i * tq + tq - 1) // tk + 1
    qi = qi_ref[0]
    _fill_head_weights(wb_ref, w_ref[0])
    qpos = i * tq + lax.broadcasted_iota(I32, (tq, 128), 0)
    lane = lax.broadcasted_iota(I32, (tq, 128), 1)
    chunk = lambda c: slice(c * 128, (c + 1) * 128)

    tpi = DSA_FILL_TILES if nt % DSA_FILL_TILES == 0 else 1

    def fill(it, carry):
        for u in range(tpi):
            t = it * tpi + u

            def emit(g, c, key, t=t):
                rows = slice(g * DSA_QG, (g + 1) * DSA_QG)
                kpos = t * tk + c * 128 + lax.broadcasted_iota(I32, key.shape, 1)
                key = jnp.where(kpos <= i * tq + g * DSA_QG + lax.broadcasted_iota(I32, key.shape, 0), key, INT_MIN)
                keys_ref[t, rows, chunk(c)] = key
                h16_ref[t, rows, chunk(c)] = (key >> 16).astype(I16)
            _score_keys(qi, wb_ref, ki_ref[0, pl.ds(pl.multiple_of(t * tk, tk), tk), :], emit)
        return carry

    nit = (nkt + tpi - 1) // tpi
    lax.fori_loop(0, nit, fill, 0)
    kf = float(topk)

    def tile_loop(body, init):
        def group(it, carry):
            for u in range(tpi):
                carry = body(it * tpi + u, carry)
            return carry
        return lax.fori_loop(0, nit, group, init)

    def search16():
        def count16(cand):
            def body(t, acc):
                for c in range(nch):
                    acc = acc + jnp.where(h16_ref[t, :, chunk(c)] >= cand, jnp.int16(1), jnp.int16(0))
                return acc
            acc = tile_loop(body, jnp.zeros((tq, 128), I16))
            return jnp.sum(acc.astype(F32), axis=-1, keepdims=True)

        v = jnp.where(count16(jnp.zeros((tq, 128), I16)) >= kf, 0, -32768).astype(I32)

        def bit_body(bi, v):
            cand = v | jnp.left_shift(jnp.int32(1), 14 - bi)
            c = count16(jnp.broadcast_to(cand, (tq, 128)).astype(I16))
            return jnp.where(c >= kf, cand, v)
        return lax.fori_loop(0, 15, bit_body, v)

    thr_hi = search16()
    hi_b = jnp.broadcast_to(thr_hi, (tq, 128))

    def lower_half(t, carry):
        for c in range(nch):
            k = keys_ref[t, :, chunk(c)]
            hi = k >> 16
            lo = (k & 0xFFFF) - 32768
            h16_ref[t, :, chunk(c)] = jnp.where(hi == hi_b, lo, jnp.where(hi > hi_b, 32767, -32768)).astype(I16)
        return carry

    tile_loop(lower_half, 0)
    thr =jnp.left_shift(thr_hi, 16) + (search16() + 32768)
    thr_b = jnp.broadcast_to(thr, (tq, 128))

    def count(pred):
        def body(t, acc):
            for c in range(nch):
                acc = acc + jnp.where(pred(keys_ref[t, :, chunk(c)], t * tk + c * 128), 1.0, 0.0)
            return acc
        acc = tile_loop(body, jnp.zeros((tq, 128), F32))
        return jnp.sum(acc, axis=-1, keepdims=True)

    need = kf - count(lambda k, base: k > thr_b)
    n_eq = count(lambda k, base: k == thr_b)
    cut_ref[...] = jnp.full((tq, 128), seq, I32)
    partial = jnp.logical_and(n_eq > need, thr > INT_MIN)

    @pl.when(jnp.max(jnp.where(partial, 1.0, 0.0)) > 0.0)
    def _():
        def idx_body(bi, cut):
            cand = cut | jnp.left_shift(jnp.int32(1), (seq.bit_length() - 1) - bi)
            cand_b = jnp.broadcast_to(cand, (tq, 128))
            c = count(lambda k, base: jnp.logical_and(k == thr_b, base + lane < cand_b))
            return jnp.where(c < need, cand, cut)
        cut = lax.fori_loop(0, seq.bit_length(), idx_body, jnp.zeros((tq, 1), I32))
        cut_ref[...] = jnp.broadcast_to(cut, (tq, 128))

    cut_b = jnp.where(thr_b == INT_MIN, qpos, cut_ref[...])

    def tile_copy(t):
        return pltpu.make_async_copy(stage_ref.at[t], mask_hbm.at[b, i, t], sem.at[0])

    @pl.when(jnp.logical_or(b > 0, i > 0))
    def _():
        for t in range(nt):
            tile_copy(t).wait()

    def mask_tile(t, carry):
        for c in range(nch):
            k = keys_ref[t, :, chunk(c)]
            sel = jnp.logical_or(k > thr_b, jnp.logical_and(k == thr_b, t * tk + c * 128 + lane <= cut_b))
            stage_ref[t, :, chunk(c)] = jnp.where(sel, 0.0, NEG_BIG).astype(stage_ref.dtype)
        return carry

    tile_loop(mask_tile, 0)

    def dead_tile(t, carry):
        stage_ref[t] = jnp.full((tq, tk), NEG_BIG, stage_ref.dtype)
        return carry

    lax.fori_loop(nit * tpi, nt, dead_tile, 0)
    for t in range(nt):
        tile_copy(t).start()

    @pl.when(jnp.logical_and(b == pl.num_programs(0) - 1, i == pl.num_programs(1) - 1))
    def _():
        for t in range(nt):
            tile_copy(t).wait()


def _select(qi_rot, w_s, ki_rot, topk):
    bsz, n_grp, n_pair, qg, _ = qi_rot.shape
    s = n_grp * qg
    tq, tk = DSA_TQ_SEL, DSA_TK
    return pl.pallas_call(
        functools.partial(_select_kernel, topk=topk, seq=s), name="dsa_select",
        grid=(bsz, s // tq),
        in_specs=[pl.BlockSpec((1, tq // qg, n_pair, qg, 128), lambda b, i: (b, i, 0, 0, 0)),
                  pl.BlockSpec((1, tq, 128), lambda b, i: (b, i, 0)),
                  pl.BlockSpec((1, s, 256), lambda b, i: (b, 0, 0))],
        out_specs=pl.BlockSpec(memory_space=pl.ANY),
        out_shape=jax.ShapeDtypeStruct((bsz, s // tq, s // tk, tq, tk), BF16),
        scratch_shapes=[pltpu.VMEM((s // tk, tq, tk), I32),
                        pltpu.VMEM((s // tk, tq, tk), I16),
                        pltpu.VMEM((IDX_N_HEADS, tq, 128), F32),
                        pltpu.VMEM((tq, 128), I32),
                        pltpu.VMEM((s // tk, tq, tk), BF16),
                        pltpu.SemaphoreType.DMA((1,))],
        compiler_params=_cparams(("arbitrary", "arbitrary")),
    )(qi_rot, w_s, ki_rot)


def _attn_kernel(qb_ref, kt_ref, q_ref, k_ref, v_ref, mask_ref, o_ref, m_ref, acc_ref, bias_ref, s_ref, p_ref, al_ref):
    tq, tk, rc = DSA_TQ, DSA_TK_ATT, DSA_RC
    rep = ATT_N_HEADS // ATT_N_KV_HEADS
    nch = tk // 128
    step = pl.program_id(1)
    qb = qb_ref[step]
    kt = kt_ref[step]

    @pl.when(kt == 0)
    def _():
        m_ref[...] = jnp.full(m_ref.shape, NEG_BIG, F32)
        acc_ref[...] = jnp.zeros(acc_ref.shape, F32)

    for r in range(tq // DSA_TQ_SEL):
        for c in range(tk // DSA_TK):
            bias_ref[r * DSA_TQ_SEL:(r + 1) * DSA_TQ_SEL, c * DSA_TK:(c + 1) * DSA_TK] = mask_ref[0, r, c].astype(F32)

    for g in range(ATT_N_KV_HEADS):
        qg = q_ref[0, g * rep:(g + 1) * rep].reshape(rep * tq, ATT_HEAD_DIM)
        kg = k_ref[0, :, g * 128:(g + 1) * 128]
        s_ref[...] = lax.dot_general(qg, kg, (((1,), (1,)), ((), ())), preferred_element_type=F32)

        def chunk(ci, carry):
            rows = pl.ds(pl.multiple_of(ci * rc, rc), rc)
            brows = pl.ds(pl.multiple_of((ci % (tq // rc)) * rc, rc), rc)
            sc = [s_ref[rows, c * 128:(c + 1) * 128] + bias_ref[brows, c * 128:(c + 1) * 128] for c in range(nch)]
            mx = sc[0]
            for c in range(1, nch):
                mx = jnp.maximum(mx, sc[c])
            m_old = m_ref[g, rows, :]
            m_new = jnp.maximum(m_old, jnp.max(mx, axis=-1, keepdims=True))
            al_ref[rows, :] = jnp.exp2(m_old - m_new)
            m_ref[g, rows, :] = m_new
            for c in range(nch):
                p_ref[rows, c * 128:(c + 1) * 128] = jnp.exp2(sc[c] - m_new).astype(p_ref.dtype)
            return carry

        lax.fori_loop(0, rep * tq // rc, chunk, 0, unroll=True)
        vg = v_ref[0, :, g * 256:(g + 1) * 256]
        for r in range(DSA_PV_SPLIT):
            rows = slice(r * rep * tq // DSA_PV_SPLIT, (r + 1) * rep * tq // DSA_PV_SPLIT)
            pv = jnp.dot(p_ref[rows, :], vg, preferred_element_type=F32)
            al = al_ref[rows, :]
            acc_ref[g, rows, 0:128] = al * acc_ref[g, rows, 0:128] + pv[:, 0:128]
            acc_ref[g, rows, 128:256] = al * acc_ref[g, rows, 128:256] + pv[:, 128:256]

    @pl.when(kt == (qb * tq + tq - 1) // tk)
    def _():
        for g in range(ATT_N_KV_HEADS):
            o = acc_ref[g, :, 0:128] / acc_ref[g, :, 128:256]
            for r in range(rep):
                h = g * rep + r
                o_ref[0, :, h * 128:(h + 1) * 128] = o[r * tq:(r + 1) * tq].astype(o_ref.dtype)


def _attn(q_rot, k_rot, v_ext, mask):
    bsz, n_h, s, _ = q_rot.shape
    tq, tk = DSA_TQ, DSA_TK_ATT
    nkv = ATT_N_KV_HEADS * ATT_HEAD_DIM
    pairs = [(qb, kt) for qb in range(s // tq) for kt in range((qb * tq + tq - 1) // tk + 1)]
    qb_arr = jnp.asarray([p[0] for p in pairs], I32)
    kt_arr = jnp.asarray([p[1] for p in pairs], I32)
    rep = n_h // ATT_N_KV_HEADS
    grid_spec = pltpu.PrefetchScalarGridSpec(
        num_scalar_prefetch=2,
        grid=(bsz, len(pairs)),
        in_specs=[pl.BlockSpec((1, n_h, tq, 128), lambda b, t, qb, kt: (b, 0, qb[t], 0)),
                  pl.BlockSpec((1, tk, nkv), lambda b, t, qb, kt: (b, kt[t], 0)),
                  pl.BlockSpec((1, tk, 2 * nkv), lambda b, t, qb, kt: (b, kt[t], 0)),
                  pl.BlockSpec((1, tq // DSA_TQ_SEL, tk // DSA_TK, DSA_TQ_SEL, DSA_TK),
                               lambda b, t, qb, kt: (b, qb[t], kt[t], 0, 0))],
        out_specs=pl.BlockSpec((1, tq, n_h * 128), lambda b, t, qb, kt: (b, qb[t], 0)),
        scratch_shapes=[pltpu.VMEM((ATT_N_KV_HEADS, rep * tq, 128), F32),
                        pltpu.VMEM((ATT_N_KV_HEADS, rep * tq, 256), F32),
                        pltpu.VMEM((tq, tk), F32),
                        pltpu.VMEM((rep * tq, tk), F32),
                        pltpu.VMEM((rep * tq, tk), BF16),
                        pltpu.VMEM((rep * tq, 128), F32)])
    return pl.pallas_call(
        _attn_kernel, name="dsa_attn",
        grid_spec=grid_spec,
        out_shape=jax.ShapeDtypeStruct((bsz, s, n_h * 128), BF16),
        compiler_params=_cparams(("arbitrary", "arbitrary")),
    )(qb_arr, kt_arr, q_rot, k_rot, v_ext, mask)


def _merge_kernel(y_ref, o_ref, gs_ref, ga_ref, ws_ref, wa_ref, out_ref):
    a = jnp.dot(y_ref[0], ws_ref[...], preferred_element_type=F32)
    b = jnp.dot(o_ref[0], wa_ref[...], preferred_element_type=F32)
    out_ref[0] = (_sigmoid(gs_ref[0].astype(F32)) * a + _sigmoid(ga_ref[0].astype(F32)) * b).astype(out_ref.dtype)


def _merge(y_ssm, o_att, proj, w_ssm, w_att, tm, tn):
    bsz, s, d_inner = y_ssm.shape
    d_att = o_att.shape[2]
    d = w_ssm.shape[1]
    gs_off = 2 * d_inner + 2 * SSM_N_GROUPS * SSM_D_STATE + d_att
    ga_off = gs_off + d
    return pl.pallas_call(
        _merge_kernel, name="merge",
        grid=(bsz, s // tm, d // tn),
        in_specs=[pl.BlockSpec((1, tm, d_inner), lambda b, i, j: (b, i, 0)),
                  pl.BlockSpec((1, tm, d_att), lambda b, i, j: (b, i, 0)),
                  pl.BlockSpec((1, tm, tn), lambda b, i, j: (b, i, gs_off // tn + j)),
                  pl.BlockSpec((1, tm, tn), lambda b, i, j: (b, i, ga_off // tn + j)),
                  pl.BlockSpec((d_inner, tn), lambda b, i, j: (0, j)),
                  pl.BlockSpec((d_att, tn), lambda b, i, j: (0, j))],
        out_specs=pl.BlockSpec((1, tm, tn), lambda b, i, j: (b, i, j)),
        out_shape=jax.ShapeDtypeStruct((bsz, s, d), BF16),
        compiler_params=_cparams(("arbitrary", "arbitrary", "arbitrary")),
    )(y_ssm, o_att, proj, proj, w_ssm, w_att)


def _post_kernel(m_ref, wo_ref, x_ref, mod_ref, g_ref, rw_ref, rb_ref, x1_ref, h2_ref, gate_ref, idx_ref):
    mod = mod_ref[0]
    out = jnp.dot(m_ref[0], wo_ref[...], preferred_element_type=F32)
    x1 = x_ref[0] + mod[2:3] * out
    x1_ref[0] = x1
    h2 = _modnorm(x1, g_ref[...], mod[4:5], mod[3:4])
    h2_ref[0] = h2
    logit = _dot_hi(h2, rw_ref[...]) + rb_ref[...]
    lane = lax.broadcasted_iota(I32, logit.shape, 1)
    lane_f = lane.astype(F32)
    vals, idxs = [], []
    for _ in range(TOP_K):
        m = jnp.max(logit, axis=-1, keepdims=True)
        idx = jnp.min(jnp.where(logit == m, lane_f, 128.0), axis=-1, keepdims=True)
        vals.append(m)
        idxs.append(idx)
        logit = jnp.where(lane_f == idx, -3e38, logit)
    es = [jnp.exp(v - vals[0]) for v in vals]
    denom = es[0] + es[1] + es[2] + es[3]
    gate = jnp.zeros(logit.shape, F32)
    sel = jnp.zeros(logit.shape, F32)
    for k in range(TOP_K):
        gate = jnp.where(lane == k, es[k] / denom, gate)
        sel = jnp.where(lane == k, idxs[k], sel)
    gate_ref[0] = gate
    idx_ref[0] = sel.astype(I32)


def _post(merged, w_o, x, mod, g, router_w, router_b, tm):
    bsz, s, d = x.shape
    n_e = router_w.shape[1]
    rw = jnp.zeros((d, 128), F32).at[:, :n_e].set(router_w.astype(F32))
    rb = jnp.full((1, 128), NEG_BIG, F32).at[0, :n_e].set(router_b.astype(F32))
    row = lambda w: pl.BlockSpec((1, tm, w), lambda b, i: (b, i, 0))
    return pl.pallas_call(
        _post_kernel, name="post_router",
        grid=(bsz, s // tm),
        in_specs=[row(d),
                  pl.BlockSpec((d, d), lambda b, i: (0, 0)),
                  row(d),
                  pl.BlockSpec((1, 6, d), lambda b, i: (b, 0, 0)),
                  pl.BlockSpec((1, d), lambda b, i: (0, 0)),
                  pl.BlockSpec((d, 128), lambda b, i: (0, 0)),
                  pl.BlockSpec((1, 128), lambda b, i: (0, 0))],
        out_specs=[row(d), row(d), row(128), row(128)],
        out_shape=[jax.ShapeDtypeStruct((bsz, s, d), F32),
                   jax.ShapeDtypeStruct((bsz, s, d), F32),
                   jax.ShapeDtypeStruct((bsz, s, 128), F32),
                   jax.ShapeDtypeStruct((bsz, s, 128), I32)],
        compiler_params=_cparams(("arbitrary", "arbitrary")),
    )(merged, w_o, x, mod, g.reshape(1, d), rw, rb)


MOE_TM = 512
MOE_TN = 512


def _moe_kernel(be_ref, nu_ref, rt_ref, rtn_ref, rd_ref, h_hbm, wg_ref, wu_ref, bg_ref, bu_ref, wd_ref, bd_ref,
                y_hbm, xf_ref, xb_ref, acc_ref, yo_ref, sem):
    tm = MOE_TM
    blk = pl.program_id(0)
    j = pl.program_id(1)
    n_j = pl.num_programs(1)
    n_used = nu_ref[0]

    def hbm_row(ref, r):
        return ref.at[lax.shift_right_logical(r, 3), pl.ds(r & 7, 1)]

    def gather_start(tok_ref):
        def body(o, c):
            for u in range(8):
                pltpu.make_async_copy(hbm_row(h_hbm, tok_ref[0, 0, o * 8 + u]), xf_ref.at[o, pl.ds(u, 1)],
                                      sem.at[0]).start()
            return c
        lax.fori_loop(0, tm // 8, body, 0)

    def all_rows_in():
        return pltpu.make_async_copy(h_hbm.at[pl.ds(0, tm // 8)], xf_ref, sem.at[0])

    def all_rows_out():
        return pltpu.make_async_copy(yo_ref, y_hbm.at[pl.ds(0, tm // 8)], sem.at[1])

    @pl.when(blk < n_used)
    def _():
        @pl.when(j == 0)
        def _():
            @pl.when(blk == 0)
            def _():
                gather_start(rt_ref)
            all_rows_in().wait()
            xb_ref[...] = xf_ref[...].reshape(xb_ref.shape).astype(BF16)
            acc_ref[...] = jnp.zeros(acc_ref.shape, F32)

            @pl.when(blk + 1 < n_used)
            def _():
                gather_start(rtn_ref)

        x = xb_ref[...]
        gate = jnp.dot(x, wg_ref[0], preferred_element_type=F32) + bg_ref[0]
        up = jnp.dot(x, wu_ref[0], preferred_element_type=F32) + bu_ref[0]
        gate = jnp.minimum(gate, SWIGLU_LIMIT)
        up = jnp.clip(up, -SWIGLU_LIMIT, SWIGLU_LIMIT)
        act = (up + 1.0) * (gate * _sigmoid(SWIGLU_ALPHA * gate))
        acc_ref[...] += jnp.dot(act.astype(BF16), wd_ref[0], preferred_element_type=F32)

        @pl.when(j == n_j - 1)
        def _():
            @pl.when(blk > 0)
            def _():
                all_rows_out().wait()

            yo_ref[...] = (acc_ref[...] + bd_ref[0]).reshape(yo_ref.shape)

            def body(o, c):
                for u in range(8):
                    pltpu.make_async_copy(yo_ref.at[o, pl.ds(u, 1)], hbm_row(y_hbm, rd_ref[0, 0, o * 8 + u]),
                                          sem.at[1]).start()
                return c
            lax.fori_loop(0, tm // 8, body, 0)

            @pl.when(blk == n_used - 1)
            def _():
                all_rows_out().wait()

    @pl.when(jnp.logical_and(blk >= n_used, j == 0))
    def _():
        @pl.when(blk == n_used)
        def _():
            yo_ref[...] = jnp.zeros(yo_ref.shape, F32)
        fill = pltpu.make_async_copy(yo_ref, y_hbm.at[pl.ds(blk * (tm // 8), tm // 8)], sem.at[1])
        fill.start()
        fill.wait()


def _moe(h2, gate_idx, w_gu, b_gu, w_dn, b_dn):
    n_tok, d = h2.shape
    n_e, _, d2 = w_gu.shape
    d_e = d2 // 2
    tm, tn = MOE_TM, MOE_TN
    n_j = d_e // tn
    n_asg = n_tok * TOP_K
    e_flat = gate_idx[:, :TOP_K].reshape(-1)
    onehot = (e_flat[:, None] == jnp.arange(n_e, dtype=I32)[None, :]).astype(I32)
    csum = jnp.cumsum(onehot, axis=0)
    rank = jnp.take_along_axis(csum, e_flat[:, None], axis=1)[:, 0] - 1
    counts = csum[-1]
    padded = (counts + tm - 1) // tm * tm
    ends = jnp.cumsum(padded)
    dest = (ends - padded)[e_flat] + rank
    n_blocks = -(-(n_asg + n_e * (tm - 1)) // tm)
    row_asg = jnp.full((n_blocks * tm,), -1, I32).at[dest].set(jnp.arange(n_asg, dtype=I32))
    is_pad = row_asg < 0
    pad_rank = jnp.cumsum(is_pad.astype(I32)) - 1
    row_tok = jnp.where(is_pad, 0, row_asg // TOP_K).reshape(n_blocks, 1, tm)
    row_dst = jnp.where(is_pad, n_asg + pad_rank, (row_asg % TOP_K) * n_tok + row_asg // TOP_K)
    row_dst = row_dst.reshape(n_blocks, 1, tm)
    block_expert = jnp.minimum(
        jnp.searchsorted(ends, jnp.arange(n_blocks, dtype=I32) * tm, side='right'), n_e - 1).astype(I32)
    n_used = (ends[-1:] // tm).astype(I32)

    def jj(blk, j, nu):
        return jnp.where(blk < nu[0], j, n_j - 1)

    smem_rows = pl.BlockSpec((1, 1, tm), lambda blk, j, be, nu: (blk, 0, 0), memory_space=pltpu.SMEM)
    smem_next = pl.BlockSpec((1, 1, tm), lambda blk, j, be, nu: (jnp.minimum(blk + 1, n_blocks - 1), 0, 0),
                             memory_space=pltpu.SMEM)
    grid_spec = pltpu.PrefetchScalarGridSpec(
        num_scalar_prefetch=2,
        grid=(n_blocks, n_j),
        in_specs=[smem_rows, smem_next, smem_rows,
                  pl.BlockSpec(memory_space=pl.ANY),
                  pl.BlockSpec((1, d, tn), lambda blk, j, be, nu: (be[blk], 0, jj(blk, j, nu))),
                  pl.BlockSpec((1, d, tn), lambda blk, j, be, nu: (be[blk], 0, n_j + jj(blk, j, nu))),
                  pl.BlockSpec((1, 1, tn), lambda blk, j, be, nu: (be[blk], 0, jj(blk, j, nu))),
                  pl.BlockSpec((1, 1, tn), lambda blk, j, be, nu: (be[blk], 0, n_j + jj(blk, j, nu))),
                  pl.BlockSpec((1, tn, d), lambda blk, j, be, nu: (be[blk], jj(blk, j, nu), 0)),
                  pl.BlockSpec((1, 1, d), lambda blk, j, be, nu: (be[blk], 0, 0))],
        out_specs=pl.BlockSpec(memory_space=pl.ANY),
        scratch_shapes=[pltpu.VMEM((tm // 8, 8, d), F32),
                        pltpu.VMEM((tm, d), BF16),
                        pltpu.VMEM((tm, d), F32),
                        pltpu.VMEM((tm // 8, 8, d), F32),
                        pltpu.SemaphoreType.DMA((2,))])
    y = pl.pallas_call(
        _moe_kernel, name="moe_experts",
        grid_spec=grid_spec,
        out_shape=jax.ShapeDtypeStruct((n_blocks * tm // 8, 8, d), F32),
        compiler_params=_cparams(("arbitrary", "arbitrary")),
    )(block_expert, n_used, row_tok, row_tok, row_dst, h2.reshape(n_tok // 8, 8, d), w_gu, w_gu,
      b_gu.reshape(n_e, 1, d2), b_gu.reshape(n_e, 1, d2), w_dn, b_dn.reshape(n_e, 1, d))
    return y.reshape(n_blocks * tm, d)


def _combine_kernel(x1_ref, y0_ref, y1_ref, y2_ref, y3_ref, gate_ref, mod_ref, g_ref, o_ref, *, final):
    gate = gate_ref[0]
    moe = gate[:, 0:1] * y0_ref[...]
    for k, y_ref in ((1, y1_ref), (2, y2_ref), (3, y3_ref)):
        moe = moe + gate[:, k:k + 1] * y_ref[...]
    x2 = x1_ref[0] + mod_ref[0][5:6] * moe
    if final:
        ms = jnp.mean(x2 * x2, axis=-1, keepdims=True)
        x2 = x2 * lax.rsqrt(ms + NORM_EPS) * g_ref[...]
    o_ref[0] = x2


def _combine(x1, y, gates, mod, g, final, tm):
    bsz, s, d = x1.shape
    assert TOP_K == 4
    row = lambda w: pl.BlockSpec((1, tm, w), lambda b, i: (b, i, 0))
    nb = s // tm

    def slot(k):
        return pl.BlockSpec((tm, d), lambda b, i: (k * bsz * nb + b * nb + i, 0))

    return pl.pallas_call(
        functools.partial(_combine_kernel, final=final), name="combine",
        grid=(bsz, nb),
        in_specs=[row(d), slot(0), slot(1), slot(2), slot(3), row(128),
                  pl.BlockSpec((1, 6, d), lambda b, i: (b, 0, 0)),
                  pl.BlockSpec((1, d), lambda b, i: (0, 0))],
        out_specs=row(d),
        out_shape=jax.ShapeDtypeStruct((bsz, s, d), F32),
        compiler_params=_cparams(("arbitrary", "arbitrary")),
    )(x1, y, y, y, y, gates, mod, g.reshape(1, d))


def _pick(n, *cands):
    for c in cands:
        if n % c == 0:
            return c
    return n


def kernel(x, c, positions, ada_w, ada_b, norm_mix_g, w_in, conv_w, conv_b, dt_bias, a_log, d_skip, ssm_norm_g,
           w_out_ssm, w_out_att, w_o, norm_ffn_g, router_w, router_b, expert_w_gate_up, expert_b_gate_up,
           expert_w_down, expert_b_down, final_norm_g):
    bsz, s, d = x.shape
    depth = ada_w.shape[0]
    topk = min(IDX_TOPK_MAX, s // 4)
    for l in range(depth):
        mod = _adaln(c, ada_w[l], ada_b[l]).reshape(bsz, 6, d)
        w_big, w_small = _pack_w_in(w_in[l])
        tm = _pick(s, 1024, 512, 256)
        proj = _inproj(x, norm_mix_g[l], mod, w_big, BF16, tm, 1280)
        small = _inproj(x, norm_mix_g[l], mod, w_small, F32, tm, w_small.shape[1])
        y_ssm = _ssd(proj, small, conv_w[l], conv_b[l], dt_bias[l], a_log[l], d_skip[l], ssm_norm_g[l], 2 * d)
        q_rot, k_rot, v_ext, qi_rot, ki_rot, w_s = _rope(proj, small, positions, d, _pick(s, 512, 256))
        mask = _select(qi_rot, w_s, ki_rot, topk)
        o_att = _attn(q_rot, k_rot, v_ext, mask)
        merged = _merge(y_ssm, o_att, proj, w_out_ssm[l].astype(BF16), w_out_att[l].astype(BF16),
                        _pick(s, 512, 256), 1024)
        x1, h2, gates, gate_idx = _post(merged, w_o[l].astype(BF16), x, mod, norm_ffn_g[l], router_w[l],
                                        router_b[l], 256)
        y = _moe(h2.reshape(bsz * s, d), gate_idx.reshape(bsz * s, 128), expert_w_gate_up[l].astype(BF16),
                 expert_b_gate_up[l], expert_w_down[l].astype(BF16), expert_b_down[l])
        x = _combine(x1, y, gates, mod, final_norm_g, l == depth - 1, 256)
    return x
```

```python
import functools

import jax
import jax.numpy as jnp
import numpy as np
from jax import lax
from jax.experimental import pallas as pl
from jax.experimental.pallas import tpu as pltpu

F32 = jnp.float32
BF16 = jnp.bfloat16
I32 = jnp.int32
I16 = jnp.int16

SSM_HEAD_DIM = 64
SSM_N_GROUPS = 8
SSM_D_STATE = 128
SSM_CONV = 4
SSM_CHUNK = 128
ATT_N_HEADS = 16
ATT_N_KV_HEADS = 4
ATT_HEAD_DIM = 128
IDX_N_HEADS = 8
IDX_HEAD_DIM = 64
IDX_TOPK_MAX = 256
ROPE_THETA = 10000.0
N_EXPERTS = 32
TOP_K = 4
SWIGLU_ALPHA = 1.702
SWIGLU_LIMIT = 7.0
NORM_EPS = 1e-6

INT_MIN = -2147483648
NEG_BIG = -1e30
VMEM_LIMIT = 56 * 1024 * 1024


def _cparams(sem):
    return pltpu.CompilerParams(dimension_semantics=sem, vmem_limit_bytes=VMEM_LIMIT)


def _split3(a):
    hi = a.astype(BF16)
    r1 = a - hi.astype(F32)
    mid = r1.astype(BF16)
    lo = (r1 - mid.astype(F32)).astype(BF16)
    return hi, mid, lo


def _dot_x3(a, b_bf16, dims=None):
    out = None
    for p in _split3(a):
        if dims is None:
            t = jnp.dot(p, b_bf16, preferred_element_type=F32)
        else:
            t = lax.dot_general(p, b_bf16, dims, preferred_element_type=F32)
        out = t if out is None else out + t
    return out


def _dot_hi(a, b):
    a0, a1, a2 = _split3(a)
    b0, b1, b2 = _split3(b)
    d = functools.partial(jnp.dot, preferred_element_type=F32)
    return (d(a0, b0) + (d(a0, b1) + d(a1, b0))
            + (d(a0, b2) + d(a2, b0) + d(a1, b1)))


def _sigmoid(x):
    return 1.0 / (1.0 + jnp.exp(-x))


def _silu(x):
    return x * _sigmoid(x)


def _adaln_kernel(c_ref, w_ref, b_ref, o_ref):
    c = c_ref[...]
    o_ref[...] = _dot_hi(_silu(c), w_ref[...]) + b_ref[...]


def _adaln(c, ada_w, ada_b):
    bsz, d = c.shape
    n = ada_w.shape[1]
    rows = 8
    cp = jnp.zeros((rows, d), F32).at[:bsz].set(c)
    tn = 1024
    out = pl.pallas_call(
        _adaln_kernel, name="adaln",
        grid=(n // tn,),
        in_specs=[pl.BlockSpec((rows, d), lambda j: (0, 0)),
                  pl.BlockSpec((d, tn), lambda j: (0, j)),
                  pl.BlockSpec((1, tn), lambda j: (0, j))],
        out_specs=pl.BlockSpec((rows, tn), lambda j: (0, j)),
        out_shape=jax.ShapeDtypeStruct((rows, n), F32),
        compiler_params=_cparams(("arbitrary",)),
    )(cp, ada_w, ada_b.reshape(1, n))
    return out[:bsz]


def _modnorm(x, g, sc, sh):
    ms = jnp.mean(x * x, axis=-1, keepdims=True)
    y = x * lax.rsqrt(ms + NORM_EPS) * g
    return y * (1.0 + sc) + sh


def _inproj_kernel(x_ref, g_ref, mod_ref, w_ref, o_ref, h_ref, *, sc_row, sh_row):
    @pl.when(pl.program_id(2) == 0)
    def _():
        m = mod_ref[0]
        h = _modnorm(x_ref[0], g_ref[...], m[sc_row:sc_row + 1], m[sh_row:sh_row + 1])
        h_ref[...] = h.astype(BF16)

    o_ref[0] = jnp.dot(h_ref[...], w_ref[...], preferred_element_type=F32).astype(o_ref.dtype)


def _inproj(x, g, mod, w_bf16, out_dtype, tm, tn):
    bsz, s, d = x.shape
    n = w_bf16.shape[1]
    return pl.pallas_call(
        functools.partial(_inproj_kernel, sc_row=1, sh_row=0), name="inproj",
        grid=(bsz, s // tm, n // tn),
        in_specs=[pl.BlockSpec((1, tm, d), lambda b, i, j: (b, i, 0)),
                  pl.BlockSpec((1, d), lambda b, i, j: (0, 0)),
                  pl.BlockSpec((1, 6, d), lambda b, i, j: (b, 0, 0)),
                  pl.BlockSpec((d, tn), lambda b, i, j: (0, j))],
        out_specs=pl.BlockSpec((1, tm, tn), lambda b, i, j: (b, i, j)),
        out_shape=jax.ShapeDtypeStruct((bsz, s, n), out_dtype),
        scratch_shapes=[pltpu.VMEM((tm, d), BF16)],
        compiler_params=_cparams(("arbitrary", "arbitrary", "arbitrary")),
    )(x, g.reshape(1, d), mod, w_bf16)


def _ssd_kernel(z_ref, xs_ref, bm_ref, cm_ref, dt_ref, cw_ref, cb_ref, dtb_ref, alog_ref,
                dsk_ref, ng_ref, e_ref, o_ref, ubuf_ref, state_ref, y_ref):
    L = SSM_CHUNK
    G = SSM_N_GROUPS
    N = SSM_D_STATE
    P = SSM_HEAD_DIM
    d_inner = xs_ref.shape[2]
    H = d_inner // P
    R = H // G
    GW = R * P
    c_idx = pl.program_id(1)

    @pl.when(c_idx == 0)
    def _():
        ubuf_ref[0:8, :] = jnp.zeros((8, ubuf_ref.shape[1]), F32)
        state_ref[...] = jnp.zeros(state_ref.shape, F32)

    ubuf_ref[8:8 + L, 0:d_inner] = xs_ref[0].astype(F32)
    ubuf_ref[8:8 + L, d_inner:d_inner + G * N] = bm_ref[0].astype(F32)
    ubuf_ref[8:8 + L, d_inner + G * N:] = cm_ref[0].astype(F32)
    conv = cb_ref[...]
    for j in range(SSM_CONV):
        conv = conv + cw_ref[j:j + 1, :] * ubuf_ref[5 + j:5 + j + L, :]
    ubuf_ref[0:8, :] = ubuf_ref[L:L + 8, :]
    conv = _silu(conv)
    xs = conv[:, :d_inner]
    bmat = conv[:, d_inner:d_inner + G * N].astype(BF16)
    cmat = conv[:, d_inner + G * N:].astype(BF16)

    dtx = dt_ref[0][:, :H] + dtb_ref[...]
    dt = jnp.maximum(dtx, 0.0) + jnp.log(1.0 + jnp.exp(-jnp.abs(dtx)))
    a_neg = -jnp.exp(alog_ref[...])
    d_a = dt * a_neg

    row = lax.broadcasted_iota(I32, (L, L), 0)
    col = lax.broadcasted_iota(I32, (L, L), 1)
    causal = col <= row
    lmat = jnp.where(causal, 1.0, 0.0).astype(BF16)
    acum = None
    for p in _split3(d_a):
        t = jnp.dot(lmat, p, preferred_element_type=F32)
        acum = t if acum is None else acum + t
    acum_t = jnp.transpose(acum)

    e_mat = e_ref[...]
    dt_e = _dot_x3(dt, e_mat)
    acum_e = _dot_x3(acum, e_mat)
    exp_a_e = jnp.exp(acum_e)
    to_end_e = jnp.exp(acum_e[L - 1:L, :] - acum_e)

    xdt = xs * dt_e
    xdt_b = xdt.astype(BF16)
    xend_b = (xdt * to_end_e).astype(BF16)

    for g in range(G):
        b_g = bmat[:, g * N:(g + 1) * N]
        c_g = cmat[:, g * N:(g + 1) * N]
        cb = lax.dot_general(c_g, b_g, (((1,), (1,)), ((), ())), preferred_element_type=F32)
        st = state_ref[g]
        y_off = jnp.dot(c_g, st.astype(BF16), preferred_element_type=F32)
        y_ref[:, g * GW:(g + 1) * GW] = y_off * exp_a_e[:, g * GW:(g + 1) * GW]
        upd = lax.dot_general(b_g, xend_b[:, g * GW:(g + 1) * GW], (((0,), (0,)), ((), ())),
                              preferred_element_type=F32)
        state_ref[g] = st * exp_a_e[L - 1:L, g * GW:(g + 1) * GW] + upd
        for r in range(R):
            h = g * R + r
            seg = acum[:, h:h + 1] - acum_t[h:h + 1, :]
            decay = jnp.where(causal, jnp.exp(jnp.minimum(seg, 0.0)), 0.0)
            m = (cb * decay).astype(BF16)
            y_h = jnp.dot(m, xdt_b[:, h * P:(h + 1) * P], preferred_element_type=F32)
            y_ref[:, h * P:(h + 1) * P] += y_h

    y = y_ref[...] + xs * dsk_ref[...]
    yg = y * _silu(z_ref[0].astype(F32))
    for g in range(G):
        blk = yg[:, g * GW:(g + 1) * GW]
        ms = jnp.mean(blk * blk, axis=-1, keepdims=True)
        o_ref[0, :, g * GW:(g + 1) * GW] = (
            blk * lax.rsqrt(ms + NORM_EPS) * ng_ref[:, g * GW:(g + 1) * GW]).astype(o_ref.dtype)


def _ssd(proj, small, conv_w, conv_b, dt_bias, a_log, d_skip, norm_g, d_inner):
    bsz, s, _ = proj.shape
    L = SSM_CHUNK
    G, N, P = SSM_N_GROUPS, SSM_D_STATE, SSM_HEAD_DIM
    H = d_inner // P
    cdim = d_inner + 2 * G * N
    gw = d_inner // G
    zb = d_inner // d_inner
    bc0 = 2 * d_inner // (G * N)
    e_mat = (jnp.arange(d_inner, dtype=I32)[None, :] // P == jnp.arange(H, dtype=I32)[:, None]).astype(BF16)
    return pl.pallas_call(
        _ssd_kernel, name="ssd",
        grid=(bsz, s // L),
        in_specs=[pl.BlockSpec((1, L, d_inner), lambda b, c: (b, c, 0)),
                  pl.BlockSpec((1, L, d_inner), lambda b, c: (b, c, zb)),
                  pl.BlockSpec((1, L, G * N), lambda b, c: (b, c, bc0)),
                  pl.BlockSpec((1, L, G * N), lambda b, c: (b, c, bc0 + 1)),
                  pl.BlockSpec((1, L, 128), lambda b, c: (b, c, 2)),
                  pl.BlockSpec((SSM_CONV, cdim), lambda b, c: (0, 0)),
                  pl.BlockSpec((1, cdim), lambda b, c: (0, 0)),
                  pl.BlockSpec((1, H), lambda b, c: (0, 0)),
                  pl.BlockSpec((1, H), lambda b, c: (0, 0)),
                  pl.BlockSpec((1, d_inner), lambda b, c: (0, 0)),
                  pl.BlockSpec((1, d_inner), lambda b, c: (0, 0)),
                  pl.BlockSpec((H, d_inner), lambda b, c: (0, 0))],
        out_specs=pl.BlockSpec((1, L, d_inner), lambda b, c: (b, c, 0)),
        out_shape=jax.ShapeDtypeStruct((bsz, s, d_inner), BF16),
        scratch_shapes=[pltpu.VMEM((L + 8, cdim), F32),
                        pltpu.VMEM((G, N, gw), F32),
                        pltpu.VMEM((L, d_inner), F32)],
        compiler_params=_cparams(("arbitrary", "arbitrary")),
    )(proj, proj, proj, proj, small, conv_w, conv_b.reshape(1, cdim), dt_bias.reshape(1, H),
      a_log.reshape(1, H), jnp.repeat(d_skip, P).reshape(1, d_inner), norm_g.reshape(1, d_inner), e_mat)


def _pack_w_in(w_in):
    d = w_in.shape[0]
    d_inner = 2 * d
    gn = SSM_N_GROUPS * SSM_D_STATE
    n_h = d_inner // SSM_HEAD_DIM
    sizes = (d_inner, d_inner + 2 * gn, n_h, ATT_N_HEADS * ATT_HEAD_DIM, ATT_N_KV_HEADS * ATT_HEAD_DIM,
             ATT_N_KV_HEADS * ATT_HEAD_DIM, IDX_N_HEADS * IDX_HEAD_DIM, IDX_HEAD_DIM, IDX_N_HEADS, d, d)
    pts = [int(p) for p in np.cumsum(sizes)[:-1]]
    z, xbc, dt, q, k, v, qi, ki, wi, g_ssm, g_att = jnp.split(w_in, pts, axis=1)
    big = jnp.concatenate([z, xbc, q, g_ssm, g_att, k, v, qi], axis=1).astype(BF16)
    zeros = lambda n: jnp.zeros((d, n), w_in.dtype)
    small = jnp.concatenate([ki, zeros(128 - IDX_HEAD_DIM), wi, zeros(128 - IDX_N_HEADS), dt, zeros(128 - n_h)],
                            axis=1).astype(BF16)
    return big, small


def _rope_kernel(q_ref, k_ref, v_ref, qi_ref, sm_ref, wi_ref, ca_ref, sa_ref, ci_ref, si_ref,
                 qo_ref, ko_ref, vo_ref, qio_ref, kio_ref, wo_ref):
    ca = ca_ref[0]
    sa = sa_ref[0]
    ci = ci_ref[0]
    si = si_ref[0]
    att_scale = ATT_HEAD_DIM ** -0.5 * 1.4426950408889634
    q = q_ref[0].astype(F32)
    for h in range(ATT_N_HEADS):
        x = q[:, h * 128:(h + 1) * 128]
        qo_ref[0, h] = ((x * ca + pltpu.roll(x, 64, 1) * sa) * att_scale).astype(qo_ref.dtype)
    k = k_ref[0].astype(F32)
    for h in range(ATT_N_KV_HEADS):
        x = k[:, h * 128:(h + 1) * 128]
        ko_ref[0, :, h * 128:(h + 1) * 128] = (x * ca + pltpu.roll(x, 64, 1) * sa).astype(ko_ref.dtype)
    lane = lax.broadcasted_iota(I32, ci.shape, 1)
    first_half = (lane % IDX_HEAD_DIM) < (IDX_HEAD_DIM // 2)

    def rot_i(x):
        sw = jnp.where(first_half, pltpu.roll(x, 128 - IDX_HEAD_DIM // 2, 1), pltpu.roll(x, IDX_HEAD_DIM // 2, 1))
        return x * ci + sw * si

    qi = qi_ref[0].astype(F32)
    for j in range(IDX_N_HEADS * IDX_HEAD_DIM // 128):
        r = rot_i(qi[:, j * 128:(j + 1) * 128]).astype(qio_ref.dtype)
        for g in range(r.shape[0] // DSA_QG):
            qio_ref[0, g, j] = r[g * DSA_QG:(g + 1) * DSA_QG]
    ki = rot_i(sm_ref[0])
    kio_ref[0, :, 0:128] = ki.astype(kio_ref.dtype)
    kio_ref[0, :, 128:256] = pltpu.roll(ki, IDX_HEAD_DIM, 1).astype(kio_ref.dtype)
    wo_ref[0] = wi_ref[0] * (IDX_HEAD_DIM ** -0.5 * IDX_N_HEADS ** -0.5)
    v = v_ref[0]
    ones = jnp.ones((v.shape[0], 128), vo_ref.dtype)
    for h in range(ATT_N_KV_HEADS):
        vo_ref[0, :, h * 256:h * 256 + 128] = v[:, h * 128:(h + 1) * 128]
        vo_ref[0, :, h * 256 + 128:(h + 1) * 256] = ones


def _rope(proj, small, positions, d_model, tm):
    bsz, s, _ = proj.shape
    d_inner = 2 * d_model
    nq = ATT_N_HEADS * ATT_HEAD_DIM
    nkv = ATT_N_KV_HEADS * ATT_HEAD_DIM
    nqi = IDX_N_HEADS * IDX_HEAD_DIM
    q_off = 2 * d_inner + 2 * SSM_N_GROUPS * SSM_D_STATE
    k_off = q_off + nq + 2 * d_model
    qi_off = k_off + 2 * nkv

    def tables(dim):
        inv = ROPE_THETA ** (-jnp.arange(0, dim, 2, dtype=F32) / dim)
        ang = positions.astype(F32)[..., None] * inv
        return jnp.cos(ang), jnp.sin(ang)

    cos_a, sin_a = tables(ATT_HEAD_DIM)
    cos_i, sin_i = tables(IDX_HEAD_DIM)
    ca = jnp.concatenate([cos_a, cos_a], -1)
    sa = jnp.concatenate([-sin_a, sin_a], -1)
    ci = jnp.concatenate([cos_i] * 4, -1)
    si = jnp.concatenate([-sin_i, sin_i, -sin_i, sin_i], -1)
    tab = pl.BlockSpec((1, tm, 128), lambda b, i: (b, i, 0))
    return pl.pallas_call(
        _rope_kernel, name="rope",
        grid=(bsz, s // tm),
        in_specs=[pl.BlockSpec((1, tm, nq), lambda b, i: (b, i, q_off // nq)),
                  pl.BlockSpec((1, tm, nkv), lambda b, i: (b, i, k_off // nkv)),
                  pl.BlockSpec((1, tm, nkv), lambda b, i: (b, i, k_off // nkv + 1)),
                  pl.BlockSpec((1, tm, nqi), lambda b, i: (b, i, qi_off // nqi)),
                  pl.BlockSpec((1, tm, 128), lambda b, i: (b, i, 0)),
                  pl.BlockSpec((1, tm, 128), lambda b, i: (b, i, 1)),
                  tab, tab, tab, tab],
        out_specs=[pl.BlockSpec((1, ATT_N_HEADS, tm, 128), lambda b, i: (b, 0, i, 0)),
                   pl.BlockSpec((1, tm, nkv), lambda b, i: (b, i, 0)),
                   pl.BlockSpec((1, tm, 2 * nkv), lambda b, i: (b, i, 0)),
                   pl.BlockSpec((1, tm // DSA_QG, nqi // 128, DSA_QG, 128), lambda b, i: (b, i, 0, 0, 0)),
                   pl.BlockSpec((1, tm, 256), lambda b, i: (b, i, 0)),
                   tab],
        out_shape=[jax.ShapeDtypeStruct((bsz, ATT_N_HEADS, s, 128), BF16),
                   jax.ShapeDtypeStruct((bsz, s, nkv), BF16),
                   jax.ShapeDtypeStruct((bsz, s, 2 * nkv), BF16),
                   jax.ShapeDtypeStruct((bsz, s // DSA_QG, nqi // 128, DSA_QG, 128), BF16),
                   jax.ShapeDtypeStruct((bsz, s, 256), BF16),
                   jax.ShapeDtypeStruct((bsz, s, 128), F32)],
        compiler_params=_cparams(("arbitrary", "arbitrary")),
    )(proj, proj, proj, proj, small, small, ca, sa, ci, si)


DSA_TQ_SEL = 128
DSA_TQ = 256
DSA_TK = 512
DSA_TK_ATT = 512
DSA_RC = 64
DSA_QG = 16
DSA_FILL_TILES = 4


def _fill_head_weights(wb_ref, w):
    tq = w.shape[0]
    for h in range(IDX_N_HEADS):
        wb_ref[h] = jnp.broadcast_to(w[:, h:h + 1], (tq, 128))


def _score_keys(qi_groups, wb_ref, k2, emit):
    n_grp, n_pair, qg, _ = qi_groups.shape
    lhs = qi_groups.reshape(n_grp * n_pair * qg, 128)
    nt = (((1,), (1,)), ((), ()))
    l_even = lax.dot_general(lhs, k2[:, 0:128], nt, preferred_element_type=F32)
    l_odd = lax.dot_general(lhs, k2[:, 128:256], nt, preferred_element_type=F32)
    for g in range(n_grp):
        for c in range(k2.shape[0] // 128):
            sc = None
            for j in range(n_pair):
                r0 = (g * n_pair + j) * qg
                for l, h in ((l_even, 2 * j), (l_odd, 2 * j + 1)):
                    t = jnp.maximum(l[r0:r0 + qg, c * 128:(c + 1) * 128], 0.0) * wb_ref[h, g * qg:(g + 1) * qg, :]
                    sc = t if sc is None else sc + t
            bits = lax.bitcast_convert_type(sc, I32)
            bits = jnp.where(bits == INT_MIN, 0, bits)
            emit(g, c, bits ^ ((bits >> 31) & 0x7FFFFFFF))


def _select_kernel(qi_ref, w_ref, ki_ref, mask_hbm, keys_ref, h16_ref, wb_ref, cut_ref, stage_ref, sem,
                   *, topk, seq):
    tq, tk = DSA_TQ_SEL, DSA_TK
    nch = tk // 128
    nt = seq // tk
    b = pl.program_id(0)
    i = pl.program_id(1)
    nkt = (i * tq + tq - 1) // tk + 1
    qi = qi_ref[0]
    _fill_head_weights(wb_ref, w_ref[0])
    qpos = i * tq + lax.broadcasted_iota(I32, (tq, 128), 0)
    lane = lax.broadcasted_iota(I32, (tq, 128), 1)
    chunk = lambda c: slice(c * 128, (c + 1) * 128)

    tpi = DSA_FILL_TILES if nt % DSA_FILL_TILES == 0 else 1

    def fill(it, carry):
        for u in range(tpi):
            t = it * tpi + u

            def emit(g, c, key, t=t):
                rows = slice(g * DSA_QG, (g + 1) * DSA_QG)
                kpos = t * tk + c * 128 + lax.broadcasted_iota(I32, key.shape, 1)
                key = jnp.where(kpos <= i * tq + g * DSA_QG + lax.broadcasted_iota(I32, key.shape, 0), key, INT_MIN)
                keys_ref[t, rows, chunk(c)] = key
                h16_ref[t, rows, chunk(c)] = (key >> 16).astype(I16)
            _score_keys(qi, wb_ref, ki_ref[0, pl.ds(pl.multiple_of(t * tk, tk), tk), :], emit)
        return carry

    nit = (nkt + tpi - 1) // tpi
    lax.fori_loop(0, nit, fill, 0)
    kf = float(topk)

    def tile_loop(body, init):
        def group(it, carry):
            for u in range(tpi):
                carry = body(it * tpi + u, carry)
            return carry
        return lax.fori_loop(0, nit, group, init)

    def search16():
        def count16(cand):
            def body(t, acc):
                for c in range(nch):
                    acc = acc + jnp.where(h16_ref[t, :, chunk(c)] >= cand, jnp.int16(1), jnp.int16(0))
                return acc
            acc = tile_loop(body, jnp.zeros((tq, 128), I16))
            return jnp.sum(acc.astype(F32), axis=-1, keepdims=True)

        v = jnp.where(count16(jnp.zeros((tq, 128), I16)) >= kf, 0, -32768).astype(I32)

        def bit_body(bi, v):
            cand = v | jnp.left_shift(jnp.int32(1), 14 - bi)
            c = count16(jnp.broadcast_to(cand, (tq, 128)).astype(I16))
            return jnp.where(c >= kf, cand, v)
        return lax.fori_loop(0, 15, bit_body, v)

    thr_hi = search16()
    hi_b = jnp.broadcast_to(thr_hi, (tq, 128))

    def lower_half(t, carry):
        for c in range(nch):
            k = keys_ref[t, :, chunk(c)]
            hi = k >> 16
            lo = (k & 0xFFFF) - 32768
            h16_ref[t, :, chunk(c)] = jnp.where(hi == hi_b, lo, jnp.where(hi > hi_b, 32767, -32768)).astype(I16)
        return carry

    tile_loop(lower_half, 0)
    thr =jnp.left_shift(thr_hi, 16) + (search16() + 32768)
    thr_b = jnp.broadcast_to(thr, (tq, 128))

    def count(pred):
        def body(t, acc):
            for c in range(nch):
                acc = acc + jnp.where(pred(keys_ref[t, :, chunk(c)], t * tk + c * 128), 1.0, 0.0)
            return acc
        acc = tile_loop(body, jnp.zeros((tq, 128), F32))
        return jnp.sum(acc, axis=-1, keepdims=True)

    need = kf - count(lambda k, base: k > thr_b)
    n_eq = count(lambda k, base: k == thr_b)
    cut_ref[...] = jnp.full((tq, 128), seq, I32)
    partial = jnp.logical_and(n_eq > need, thr > INT_MIN)

    @pl.when(jnp.max(jnp.where(partial, 1.0, 0.0)) > 0.0)
    def _():
        def idx_body(bi, cut):
            cand = cut | jnp.left_shift(jnp.int32(1), (seq.bit_length() - 1) - bi)
            cand_b = jnp.broadcast_to(cand, (tq, 128))
            c = count(lambda k, base: jnp.logical_and(k == thr_b, base + lane < cand_b))
            return jnp.where(c < need, cand, cut)
        cut = lax.fori_loop(0, seq.bit_length(), idx_body, jnp.zeros((tq, 1), I32))
        cut_ref[...] = jnp.broadcast_to(cut, (tq, 128))

    cut_b = jnp.where(thr_b == INT_MIN, qpos, cut_ref[...])

    def tile_copy(t):
        return pltpu.make_async_copy(stage_ref.at[t], mask_hbm.at[b, i, t], sem.at[0])

    @pl.when(jnp.logical_or(b > 0, i > 0))
    def _():
        for t in range(nt):
            tile_copy(t).wait()

    def mask_tile(t, carry):
        for c in range(nch):
            k = keys_ref[t, :, chunk(c)]
            sel = jnp.logical_or(k > thr_b, jnp.logical_and(k == thr_b, t * tk + c * 128 + lane <= cut_b))
            stage_ref[t, :, chunk(c)] = jnp.where(sel, 0.0, NEG_BIG).astype(stage_ref.dtype)
        return carry

    tile_loop(mask_tile, 0)

    def dead_tile(t, carry):
        stage_ref[t] = jnp.full((tq, tk), NEG_BIG, stage_ref.dtype)
        return carry

    lax.fori_loop(nit * tpi, nt, dead_tile, 0)
    for t in range(nt):
        tile_copy(t).start()

    @pl.when(jnp.logical_and(b == pl.num_programs(0) - 1, i == pl.num_programs(1) - 1))
    def _():
        for t in range(nt):
            tile_copy(t).wait()


def _select(qi_rot, w_s, ki_rot, topk):
    bsz, n_grp, n_pair, qg, _ = qi_rot.shape
    s = n_grp * qg
    tq, tk = DSA_TQ_SEL, DSA_TK
    return pl.pallas_call(
        functools.partial(_select_kernel, topk=topk, seq=s), name="dsa_select",
        grid=(bsz, s // tq),
        in_specs=[pl.BlockSpec((1, tq // qg, n_pair, qg, 128), lambda b, i: (b, i, 0, 0, 0)),
                  pl.BlockSpec((1, tq, 128), lambda b, i: (b, i, 0)),
                  pl.BlockSpec((1, s, 256), lambda b, i: (b, 0, 0))],
        out_specs=pl.BlockSpec(memory_space=pl.ANY),
        out_shape=jax.ShapeDtypeStruct((bsz, s // tq, s // tk, tq, tk), BF16),
        scratch_shapes=[pltpu.VMEM((s // tk, tq, tk), I32),
                        pltpu.VMEM((s // tk, tq, tk), I16),
                        pltpu.VMEM((IDX_N_HEADS, tq, 128), F32),
                        pltpu.VMEM((tq, 128), I32),
                        pltpu.VMEM((s // tk, tq, tk), BF16),
                        pltpu.SemaphoreType.DMA((1,))],
        compiler_params=_cparams(("arbitrary", "arbitrary")),
    )(qi_rot, w_s, ki_rot)


def _attn_kernel(qb_ref, kt_ref, q_ref, k_ref, v_ref, mask_ref, o_ref, m_ref, acc_ref, bias_ref, s_ref, p_ref, al_ref):
    tq, tk, rc = DSA_TQ, DSA_TK_ATT, DSA_RC
    rep = ATT_N_HEADS // ATT_N_KV_HEADS
    nch = tk // 128
    step = pl.program_id(1)
    qb = qb_ref[step]
    kt = kt_ref[step]

    @pl.when(kt == 0)
    def _():
        m_ref[...] = jnp.full(m_ref.shape, NEG_BIG, F32)
        acc_ref[...] = jnp.zeros(acc_ref.shape, F32)

    for r in range(tq // DSA_TQ_SEL):
        for c in range(tk // DSA_TK):
            bias_ref[r * DSA_TQ_SEL:(r + 1) * DSA_TQ_SEL, c * DSA_TK:(c + 1) * DSA_TK] = mask_ref[0, r, c].astype(F32)

    for g in range(ATT_N_KV_HEADS):
        qg = q_ref[0, g * rep:(g + 1) * rep].reshape(rep * tq, ATT_HEAD_DIM)
        kg = k_ref[0, :, g * 128:(g + 1) * 128]
        s_ref[...] = lax.dot_general(qg, kg, (((1,), (1,)), ((), ())), preferred_element_type=F32)

        def chunk(ci, carry):
            rows = pl.ds(pl.multiple_of(ci * rc, rc), rc)
            brows = pl.ds(pl.multiple_of((ci % (tq // rc)) * rc, rc), rc)
            sc = [s_ref[rows, c * 128:(c + 1) * 128] + bias_ref[brows, c * 128:(c + 1) * 128] for c in range(nch)]
            mx = sc[0]
            for c in range(1, nch):
                mx = jnp.maximum(mx, sc[c])
            m_old = m_ref[g, rows, :]
            m_new = jnp.maximum(m_old, jnp.max(mx, axis=-1, keepdims=True))
            al_ref[rows, :] = jnp.exp2(m_old - m_new)
            m_ref[g, rows, :] = m_new
            for c in range(nch):
                p_ref[rows, c * 128:(c + 1) * 128] = jnp.exp2(sc[c] - m_new).astype(p_ref.dtype)
            return carry

        lax.fori_loop(0, rep * tq // rc, chunk, 0, unroll=True)
        pv = jnp.dot(p_ref[...], v_ref[0, :, g * 256:(g + 1) * 256], preferred_element_type=F32)
        al = al_ref[...]
        acc_ref[g, :, 0:128] = al * acc_ref[g, :, 0:128] + pv[:, 0:128]
        acc_ref[g, :, 128:256] = al * acc_ref[g, :, 128:256] + pv[:, 128:256]

    @pl.when(kt == (qb * tq + tq - 1) // tk)
    def _():
        for g in range(ATT_N_KV_HEADS):
            o = acc_ref[g, :, 0:128] / acc_ref[g, :, 128:256]
            for r in range(rep):
                h = g * rep + r
                o_ref[0, :, h * 128:(h + 1) * 128] = o[r * tq:(r + 1) * tq].astype(o_ref.dtype)


def _attn(q_rot, k_rot, v_ext, mask):
    bsz, n_h, s, _ = q_rot.shape
    tq, tk = DSA_TQ, DSA_TK_ATT
    nkv = ATT_N_KV_HEADS * ATT_HEAD_DIM
    pairs = [(qb, kt) for qb in range(s // tq) for kt in range((qb * tq + tq - 1) // tk + 1)]
    qb_arr = jnp.asarray([p[0] for p in pairs], I32)
    kt_arr = jnp.asarray([p[1] for p in pairs], I32)
    rep = n_h // ATT_N_KV_HEADS
    grid_spec = pltpu.PrefetchScalarGridSpec(
        num_scalar_prefetch=2,
        grid=(bsz, len(pairs)),
        in_specs=[pl.BlockSpec((1, n_h, tq, 128), lambda b, t, qb, kt: (b, 0, qb[t], 0)),
                  pl.BlockSpec((1, tk, nkv), lambda b, t, qb, kt: (b, kt[t], 0)),
                  pl.BlockSpec((1, tk, 2 * nkv), lambda b, t, qb, kt: (b, kt[t], 0)),
                  pl.BlockSpec((1, tq // DSA_TQ_SEL, tk // DSA_TK, DSA_TQ_SEL, DSA_TK),
                               lambda b, t, qb, kt: (b, qb[t], kt[t], 0, 0))],
        out_specs=pl.BlockSpec((1, tq, n_h * 128), lambda b, t, qb, kt: (b, qb[t], 0)),
        scratch_shapes=[pltpu.VMEM((ATT_N_KV_HEADS, rep * tq, 128), F32),
                        pltpu.VMEM((ATT_N_KV_HEADS, rep * tq, 256), F32),
                        pltpu.VMEM((tq, tk), F32),
                        pltpu.VMEM((rep * tq, tk), F32),
                        pltpu.VMEM((rep * tq, tk), BF16),
                        pltpu.VMEM((rep * tq, 128), F32)])
    return pl.pallas_call(
        _attn_kernel, name="dsa_attn",
        grid_spec=grid_spec,
        out_shape=jax.ShapeDtypeStruct((bsz, s, n_h * 128), BF16),
        compiler_params=_cparams(("arbitrary", "arbitrary")),
    )(qb_arr, kt_arr, q_rot, k_rot, v_ext, mask)


def _merge_kernel(y_ref, o_ref, gs_ref, ga_ref, ws_ref, wa_ref, out_ref):
    a = jnp.dot(y_ref[0], ws_ref[...], preferred_element_type=F32)
    b = jnp.dot(o_ref[0], wa_ref[...], preferred_element_type=F32)
    out_ref[0] = (_sigmoid(gs_ref[0].astype(F32)) * a + _sigmoid(ga_ref[0].astype(F32)) * b).astype(out_ref.dtype)


def _merge(y_ssm, o_att, proj, w_ssm, w_att, tm, tn):
    bsz, s, d_inner = y_ssm.shape
    d_att = o_att.shape[2]
    d = w_ssm.shape[1]
    gs_off = 2 * d_inner + 2 * SSM_N_GROUPS * SSM_D_STATE + d_att
    ga_off = gs_off + d
    return pl.pallas_call(
        _merge_kernel, name="merge",
        grid=(bsz, s // tm, d // tn),
        in_specs=[pl.BlockSpec((1, tm, d_inner), lambda b, i, j: (b, i, 0)),
                  pl.BlockSpec((1, tm, d_att), lambda b, i, j: (b, i, 0)),
                  pl.BlockSpec((1, tm, tn), lambda b, i, j: (b, i, gs_off // tn + j)),
                  pl.BlockSpec((1, tm, tn), lambda b, i, j: (b, i, ga_off // tn + j)),
                  pl.BlockSpec((d_inner, tn), lambda b, i, j: (0, j)),
                  pl.BlockSpec((d_att, tn), lambda b, i, j: (0, j))],
        out_specs=pl.BlockSpec((1, tm, tn), lambda b, i, j: (b, i, j)),
        out_shape=jax.ShapeDtypeStruct((bsz, s, d), BF16),
        compiler_params=_cparams(("arbitrary", "arbitrary", "arbitrary")),
    )(y_ssm, o_att, proj, proj, w_ssm, w_att)


def _post_kernel(m_ref, wo_ref, x_ref, mod_ref, g_ref, rw_ref, rb_ref, x1_ref, h2_ref, gate_ref, idx_ref):
    mod = mod_ref[0]
    out = jnp.dot(m_ref[0], wo_ref[...], preferred_element_type=F32)
    x1 = x_ref[0] + mod[2:3] * out
    x1_ref[0] = x1
    h2 = _modnorm(x1, g_ref[...], mod[4:5], mod[3:4])
    h2_ref[0] = h2
    logit = _dot_hi(h2, rw_ref[...]) + rb_ref[...]
    lane = lax.broadcasted_iota(I32, logit.shape, 1)
    lane_f = lane.astype(F32)
    vals, idxs = [], []
    for _ in range(TOP_K):
        m = jnp.max(logit, axis=-1, keepdims=True)
        idx = jnp.min(jnp.where(logit == m, lane_f, 128.0), axis=-1, keepdims=True)
        vals.append(m)
        idxs.append(idx)
        logit = jnp.where(lane_f == idx, -3e38, logit)
    es = [jnp.exp(v - vals[0]) for v in vals]
    denom = es[0] + es[1] + es[2] + es[3]
    gate = jnp.zeros(logit.shape, F32)
    sel = jnp.zeros(logit.shape, F32)
    for k in range(TOP_K):
        gate = jnp.where(lane == k, es[k] / denom, gate)
        sel = jnp.where(lane == k, idxs[k], sel)
    gate_ref[0] = gate
    idx_ref[0] = sel.astype(I32)


def _post(merged, w_o, x, mod, g, router_w, router_b, tm):
    bsz, s, d = x.shape
    n_e = router_w.shape[1]
    rw = jnp.zeros((d, 128), F32).at[:, :n_e].set(router_w.astype(F32))
    rb = jnp.full((1, 128), NEG_BIG, F32).at[0, :n_e].set(router_b.astype(F32))
    row = lambda w: pl.BlockSpec((1, tm, w), lambda b, i: (b, i, 0))
    return pl.pallas_call(
        _post_kernel, name="post_router",
        grid=(bsz, s // tm),
        in_specs=[row(d),
                  pl.BlockSpec((d, d), lambda b, i: (0, 0)),
                  row(d),
                  pl.BlockSpec((1, 6, d), lambda b, i: (b, 0, 0)),
                  pl.BlockSpec((1, d), lambda b, i: (0, 0)),
                  pl.BlockSpec((d, 128), lambda b, i: (0, 0)),
                  pl.BlockSpec((1, 128), lambda b, i: (0, 0))],
        out_specs=[row(d), row(d), row(128), row(128)],
        out_shape=[jax.ShapeDtypeStruct((bsz, s, d), F32),
                   jax.ShapeDtypeStruct((bsz, s, d), F32),
                   jax.ShapeDtypeStruct((bsz, s, 128), F32),
                   jax.ShapeDtypeStruct((bsz, s, 128), I32)],
        compiler_params=_cparams(("arbitrary", "arbitrary")),
    )(merged, w_o, x, mod, g.reshape(1, d), rw, rb)


MOE_TM = 512
MOE_TN = 512


def _moe_kernel(be_ref, nu_ref, rt_ref, rtn_ref, rd_ref, h_hbm, wg_ref, wu_ref, bg_ref, bu_ref, wd_ref, bd_ref,
                y_hbm, xf_ref, xb_ref, acc_ref, yo_ref, sem):
    tm = MOE_TM
    blk = pl.program_id(0)
    j = pl.program_id(1)
    n_j = pl.num_programs(1)
    n_used = nu_ref[0]

    def hbm_row(ref, r):
        return ref.at[lax.shift_right_logical(r, 3), pl.ds(r & 7, 1)]

    def gather_start(tok_ref):
        def body(o, c):
            for u in range(8):
                pltpu.make_async_copy(hbm_row(h_hbm, tok_ref[0, 0, o * 8 + u]), xf_ref.at[o, pl.ds(u, 1)],
                                      sem.at[0]).start()
            return c
        lax.fori_loop(0, tm // 8, body, 0)

    def all_rows_in():
        return pltpu.make_async_copy(h_hbm.at[pl.ds(0, tm // 8)], xf_ref, sem.at[0])

    def all_rows_out():
        return pltpu.make_async_copy(yo_ref, y_hbm.at[pl.ds(0, tm // 8)], sem.at[1])

    @pl.when(blk < n_used)
    def _():
        @pl.when(j == 0)
        def _():
            @pl.when(blk == 0)
            def _():
                gather_start(rt_ref)
            all_rows_in().wait()
            xb_ref[...] = xf_ref[...].reshape(xb_ref.shape).astype(BF16)
            acc_ref[...] = jnp.zeros(acc_ref.shape, F32)

            @pl.when(blk + 1 < n_used)
            def _():
                gather_start(rtn_ref)

        x = xb_ref[...]
        gate = jnp.dot(x, wg_ref[0], preferred_element_type=F32) + bg_ref[0]
        up = jnp.dot(x, wu_ref[0], preferred_element_type=F32) + bu_ref[0]
        gate = jnp.minimum(gate, SWIGLU_LIMIT)
        up = jnp.clip(up, -SWIGLU_LIMIT, SWIGLU_LIMIT)
        act = (up + 1.0) * (gate * _sigmoid(SWIGLU_ALPHA * gate))
        acc_ref[...] += jnp.dot(act.astype(BF16), wd_ref[0], preferred_element_type=F32)

        @pl.when(j == n_j - 1)
        def _():
            @pl.when(blk > 0)
            def _():
                all_rows_out().wait()

            yo_ref[...] = (acc_ref[...] + bd_ref[0]).reshape(yo_ref.shape)

            def body(o, c):
                for u in range(8):
                    pltpu.make_async_copy(yo_ref.at[o, pl.ds(u, 1)], hbm_row(y_hbm, rd_ref[0, 0, o * 8 + u]),
                                          sem.at[1]).start()
                return c
            lax.fori_loop(0, tm // 8, body, 0)

            @pl.when(blk == n_used - 1)
            def _():
                all_rows_out().wait()

    @pl.when(jnp.logical_and(blk >= n_used, j == 0))
    def _():
        @pl.when(blk == n_used)
        def _():
            yo_ref[...] = jnp.zeros(yo_ref.shape, F32)
        fill = pltpu.make_async_copy(yo_ref, y_hbm.at[pl.ds(blk * (tm // 8), tm // 8)], sem.at[1])
        fill.start()
        fill.wait()


def _moe(h2, gate_idx, w_gu, b_gu, w_dn, b_dn):
    n_tok, d = h2.shape
    n_e, _, d2 = w_gu.shape
    d_e = d2 // 2
    tm, tn = MOE_TM, MOE_TN
    n_j = d_e // tn
    n_asg = n_tok * TOP_K
    e_flat = gate_idx[:, :TOP_K].reshape(-1)
    onehot = (e_flat[:, None] == jnp.arange(n_e, dtype=I32)[None, :]).astype(I32)
    csum = jnp.cumsum(onehot, axis=0)
    rank = jnp.take_along_axis(csum, e_flat[:, None], axis=1)[:, 0] - 1
    counts = csum[-1]
    padded = (counts + tm - 1) // tm * tm
    ends = jnp.cumsum(padded)
    dest = (ends - padded)[e_flat] + rank
    n_blocks = -(-(n_asg + n_e * (tm - 1)) // tm)
    row_asg = jnp.full((n_blocks * tm,), -1, I32).at[dest].set(jnp.arange(n_asg, dtype=I32))
    is_pad = row_asg < 0
    pad_rank = jnp.cumsum(is_pad.astype(I32)) - 1
    row_tok = jnp.where(is_pad, 0, row_asg // TOP_K).reshape(n_blocks, 1, tm)
    row_dst = jnp.where(is_pad, n_asg + pad_rank, (row_asg % TOP_K) * n_tok + row_asg // TOP_K)
    row_dst = row_dst.reshape(n_blocks, 1, tm)
    block_expert = jnp.minimum(
        jnp.searchsorted(ends, jnp.arange(n_blocks, dtype=I32) * tm, side='right'), n_e - 1).astype(I32)
    n_used = (ends[-1:] // tm).astype(I32)

    def jj(blk, j, nu):
        return jnp.where(blk < nu[0], j, n_j - 1)

    smem_rows = pl.BlockSpec((1, 1, tm), lambda blk, j, be, nu: (blk, 0, 0), memory_space=pltpu.SMEM)
    smem_next = pl.BlockSpec((1, 1, tm), lambda blk, j, be, nu: (jnp.minimum(blk + 1, n_blocks - 1), 0, 0),
                             memory_space=pltpu.SMEM)
    grid_spec = pltpu.PrefetchScalarGridSpec(
        num_scalar_prefetch=2,
        grid=(n_blocks, n_j),
        in_specs=[smem_rows, smem_next, smem_rows,
                  pl.BlockSpec(memory_space=pl.ANY),
                  pl.BlockSpec((1, d, tn), lambda blk, j, be, nu: (be[blk], 0, jj(blk, j, nu))),
                  pl.BlockSpec((1, d, tn), lambda blk, j, be, nu: (be[blk], 0, n_j + jj(blk, j, nu))),
                  pl.BlockSpec((1, 1, tn), lambda blk, j, be, nu: (be[blk], 0, jj(blk, j, nu))),
                  pl.BlockSpec((1, 1, tn), lambda blk, j, be, nu: (be[blk], 0, n_j + jj(blk, j, nu))),
                  pl.BlockSpec((1, tn, d), lambda blk, j, be, nu: (be[blk], jj(blk, j, nu), 0)),
                  pl.BlockSpec((1, 1, d), lambda blk, j, be, nu: (be[blk], 0, 0))],
        out_specs=pl.BlockSpec(memory_space=pl.ANY),
        scratch_shapes=[pltpu.VMEM((tm // 8, 8, d), F32),
                        pltpu.VMEM((tm, d), BF16),
                        pltpu.VMEM((tm, d), F32),
                        pltpu.VMEM((tm // 8, 8, d), F32),
                        pltpu.SemaphoreType.DMA((2,))])
    y = pl.pallas_call(
        _moe_kernel, name="moe_experts",
        grid_spec=grid_spec,
        out_shape=jax.ShapeDtypeStruct((n_blocks * tm // 8, 8, d), F32),
        compiler_params=_cparams(("arbitrary", "arbitrary")),
    )(block_expert, n_used, row_tok, row_tok, row_dst, h2.reshape(n_tok // 8, 8, d), w_gu, w_gu,
      b_gu.reshape(n_e, 1, d2), b_gu.reshape(n_e, 1, d2), w_dn, b_dn.reshape(n_e, 1, d))
    return y.reshape(n_blocks * tm, d)


def _combine_kernel(x1_ref, y0_ref, y1_ref, y2_ref, y3_ref, gate_ref, mod_ref, g_ref, o_ref, *, final):
    gate = gate_ref[0]
    moe = gate[:, 0:1] * y0_ref[...]
    for k, y_ref in ((1, y1_ref), (2, y2_ref), (3, y3_ref)):
        moe = moe + gate[:, k:k + 1] * y_ref[...]
    x2 = x1_ref[0] + mod_ref[0][5:6] * moe
    if final:
        ms = jnp.mean(x2 * x2, axis=-1, keepdims=True)
        x2 = x2 * lax.rsqrt(ms + NORM_EPS) * g_ref[...]
    o_ref[0] = x2


def _combine(x1, y, gates, mod, g, final, tm):
    bsz, s, d = x1.shape
    assert TOP_K == 4
    row = lambda w: pl.BlockSpec((1, tm, w), lambda b, i: (b, i, 0))
    nb = s // tm

    def slot(k):
        return pl.BlockSpec((tm, d), lambda b, i: (k * bsz * nb + b * nb + i, 0))

    return pl.pallas_call(
        functools.partial(_combine_kernel, final=final), name="combine",
        grid=(bsz, nb),
        in_specs=[row(d), slot(0), slot(1), slot(2), slot(3), row(128),
                  pl.BlockSpec((1, 6, d), lambda b, i: (b, 0, 0)),
                  pl.BlockSpec((1, d), lambda b, i: (0, 0))],
        out_specs=row(d),
        out_shape=jax.ShapeDtypeStruct((bsz, s, d), F32),
        compiler_params=_cparams(("arbitrary", "arbitrary")),
    )(x1, y, y, y, y, gates, mod, g.reshape(1, d))


def _pick(n, *cands):
    for c in cands:
        if n % c == 0:
            return c
    return n


def kernel(x, c, positions, ada_w, ada_b, norm_mix_g, w_in, conv_w, conv_b, dt_bias, a_log, d_skip, ssm_norm_g,
           w_out_ssm, w_out_att, w_o, norm_ffn_g, router_w, router_b, expert_w_gate_up, expert_b_gate_up,
           expert_w_down, expert_b_down, final_norm_g):
    bsz, s, d = x.shape
    depth = ada_w.shape[0]
    topk = min(IDX_TOPK_MAX, s // 4)
    for l in range(depth):
        mod = _adaln(c, ada_w[l], ada_b[l]).reshape(bsz, 6, d)
        w_big, w_small = _pack_w_in(w_in[l])
        tm = _pick(s, 1024, 512, 256)
        proj = _inproj(x, norm_mix_g[l], mod, w_big, BF16, tm, 1280)
        small = _inproj(x, norm_mix_g[l], mod, w_small, F32, tm, w_small.shape[1])
        y_ssm = _ssd(proj, small, conv_w[l], conv_b[l], dt_bias[l], a_log[l], d_skip[l], ssm_norm_g[l], 2 * d)
        q_rot, k_rot, v_ext, qi_rot, ki_rot, w_s = _rope(proj, small, positions, d, _pick(s, 512, 256))
        mask = _select(qi_rot, w_s, ki_rot, topk)
        o_att = _attn(q_rot, k_rot, v_ext, mask)
        merged = _merge(y_ssm, o_att, proj, w_out_ssm[l].astype(BF16), w_out_att[l].astype(BF16),
                        _pick(s, 512, 256), 1024)
        x1, h2, gates, gate_idx = _post(merged, w_o[l].astype(BF16), x, mod, norm_ffn_g[l], router_w[l],
                                        router_b[l], 256)
        y = _moe(h2.reshape(bsz * s, d), gate_idx.reshape(bsz * s, 128), expert_w_gate_up[l].astype(BF16),
                 expert_b_gate_up[l], expert_w_down[l].astype(BF16), expert_b_down[l])
        x = _combine(x1, y, gates, mod, final_norm_g, l == depth - 1, 256)
    return x
```

```python
import functools

import jax
import jax.numpy as jnp
import numpy as np
from jax import lax
from jax.experimental import pallas as pl
from jax.experimental.pallas import tpu as pltpu

F32 = jnp.float32
BF16 = jnp.bfloat16
I32 = jnp.int32
I16 = jnp.int16

SSM_HEAD_DIM = 64
SSM_N_GROUPS = 8
SSM_D_STATE = 128
SSM_CONV = 4
SSM_CHUNK = 128
ATT_N_HEADS = 16
ATT_N_KV_HEADS = 4
ATT_HEAD_DIM = 128
IDX_N_HEADS = 8
IDX_HEAD_DIM = 64
IDX_TOPK_MAX = 256
ROPE_THETA = 10000.0
N_EXPERTS = 32
TOP_K = 4
SWIGLU_ALPHA = 1.702
SWIGLU_LIMIT = 7.0
NORM_EPS = 1e-6

INT_MIN = -2147483648
NEG_BIG = -1e30
VMEM_LIMIT = 56 * 1024 * 1024


def _cparams(sem):
    return pltpu.CompilerParams(dimension_semantics=sem, vmem_limit_bytes=VMEM_LIMIT)


def _split3(a):
    hi = a.astype(BF16)
    r1 = a - hi.astype(F32)
    mid = r1.astype(BF16)
    lo = (r1 - mid.astype(F32)).astype(BF16)
    return hi, mid, lo


def _dot_x3(a, b_bf16, dims=None):
    out = None
    for p in _split3(a):
        if dims is None:
            t = jnp.dot(p, b_bf16, preferred_element_type=F32)
        else:
            t = lax.dot_general(p, b_bf16, dims, preferred_element_type=F32)
        out = t if out is None else out + t
    return out


def _dot_hi(a, b):
    a0, a1, a2 = _split3(a)
    b0, b1, b2 = _split3(b)
    d = functools.partial(jnp.dot, preferred_element_type=F32)
    return (d(a0, b0) + (d(a0, b1) + d(a1, b0))
            + (d(a0, b2) + d(a2, b0) + d(a1, b1)))


def _sigmoid(x):
    return 1.0 / (1.0 + jnp.exp(-x))


def _silu(x):
    return x * _sigmoid(x)


def _adaln_kernel(c_ref, w_ref, b_ref, o_ref):
    c = c_ref[...]
    o_ref[...] = _dot_hi(_silu(c), w_ref[...]) + b_ref[...]


def _adaln(c, ada_w, ada_b):
    bsz, d = c.shape
    n = ada_w.shape[1]
    rows = 8
    cp = jnp.zeros((rows, d), F32).at[:bsz].set(c)
    tn = 1024
    out = pl.pallas_call(
        _adaln_kernel, name="adaln",
        grid=(n // tn,),
        in_specs=[pl.BlockSpec((rows, d), lambda j: (0, 0)),
                  pl.BlockSpec((d, tn), lambda j: (0, j)),
                  pl.BlockSpec((1, tn), lambda j: (0, j))],
        out_specs=pl.BlockSpec((rows, tn), lambda j: (0, j)),
        out_shape=jax.ShapeDtypeStruct((rows, n), F32),
        compiler_params=_cparams(("arbitrary",)),
    )(cp, ada_w, ada_b.reshape(1, n))
    return out[:bsz]


def _modnorm(x, g, sc, sh):
    ms = jnp.mean(x * x, axis=-1, keepdims=True)
    y = x * lax.rsqrt(ms + NORM_EPS) * g
    return y * (1.0 + sc) + sh


def _inproj_kernel(x_ref, g_ref, mod_ref, w_ref, o_ref, h_ref, *, sc_row, sh_row):
    @pl.when(pl.program_id(2) == 0)
    def _():
        m = mod_ref[0]
        h = _modnorm(x_ref[0], g_ref[...], m[sc_row:sc_row + 1], m[sh_row:sh_row + 1])
        h_ref[...] = h.astype(BF16)

    o_ref[0] = jnp.dot(h_ref[...], w_ref[...], preferred_element_type=F32).astype(o_ref.dtype)


def _inproj(x, g, mod, w_bf16, out_dtype, tm, tn):
    bsz, s, d = x.shape
    n = w_bf16.shape[1]
    return pl.pallas_call(
        functools.partial(_inproj_kernel, sc_row=1, sh_row=0), name="inproj",
        grid=(bsz, s // tm, n // tn),
        in_specs=[pl.BlockSpec((1, tm, d), lambda b, i, j: (b, i, 0)),
                  pl.BlockSpec((1, d), lambda b, i, j: (0, 0)),
                  pl.BlockSpec((1, 6, d), lambda b, i, j: (b, 0, 0)),
                  pl.BlockSpec((d, tn), lambda b, i, j: (0, j))],
        out_specs=pl.BlockSpec((1, tm, tn), lambda b, i, j: (b, i, j)),
        out_shape=jax.ShapeDtypeStruct((bsz, s, n), out_dtype),
        scratch_shapes=[pltpu.VMEM((tm, d), BF16)],
        compiler_params=_cparams(("arbitrary", "arbitrary", "arbitrary")),
    )(x, g.reshape(1, d), mod, w_bf16)


def _ssd_kernel(z_ref, xs_ref, bm_ref, cm_ref, dt_ref, cw_ref, cb_ref, dtb_ref, alog_ref,
                dsk_ref, ng_ref, e_ref, o_ref, ubuf_ref, state_ref, y_ref):
    L = SSM_CHUNK
    G = SSM_N_GROUPS
    N = SSM_D_STATE
    P = SSM_HEAD_DIM
    d_inner = xs_ref.shape[2]
    H = d_inner // P
    R = H // G
    GW = R * P
    c_idx = pl.program_id(1)

    @pl.when(c_idx == 0)
    def _():
        ubuf_ref[0:8, :] = jnp.zeros((8, ubuf_ref.shape[1]), F32)
        state_ref[...] = jnp.zeros(state_ref.shape, F32)

    ubuf_ref[8:8 + L, 0:d_inner] = xs_ref[0].astype(F32)
    ubuf_ref[8:8 + L, d_inner:d_inner + G * N] = bm_ref[0].astype(F32)
    ubuf_ref[8:8 + L, d_inner + G * N:] = cm_ref[0].astype(F32)
    conv = cb_ref[...]
    for j in range(SSM_CONV):
        conv = conv + cw_ref[j:j + 1, :] * ubuf_ref[5 + j:5 + j + L, :]
    ubuf_ref[0:8, :] = ubuf_ref[L:L + 8, :]
    conv = _silu(conv)
    xs = conv[:, :d_inner]
    bmat = conv[:, d_inner:d_inner + G * N].astype(BF16)
    cmat = conv[:, d_inner + G * N:].astype(BF16)

    dtx = dt_ref[0][:, :H] + dtb_ref[...]
    dt = jnp.maximum(dtx, 0.0) + jnp.log(1.0 + jnp.exp(-jnp.abs(dtx)))
    a_neg = -jnp.exp(alog_ref[...])
    d_a = dt * a_neg

    row = lax.broadcasted_iota(I32, (L, L), 0)
    col = lax.broadcasted_iota(I32, (L, L), 1)
    causal = col <= row
    lmat = jnp.where(causal, 1.0, 0.0).astype(BF16)
    acum = None
    for p in _split3(d_a):
        t = jnp.dot(lmat, p, preferred_element_type=F32)
        acum = t if acum is None else acum + t
    acum_t = jnp.transpose(acum)

    e_mat = e_ref[...]
    dt_e = _dot_x3(dt, e_mat)
    acum_e = _dot_x3(acum, e_mat)
    exp_a_e = jnp.exp(acum_e)
    to_end_e = jnp.exp(acum_e[L - 1:L, :] - acum_e)

    xdt = xs * dt_e
    xdt_b = xdt.astype(BF16)
    xend_b = (xdt * to_end_e).astype(BF16)

    for g in range(G):
        b_g = bmat[:, g * N:(g + 1) * N]
        c_g = cmat[:, g * N:(g + 1) * N]
        cb = lax.dot_general(c_g, b_g, (((1,), (1,)), ((), ())), preferred_element_type=F32)
        st = state_ref[g]
        y_off = jnp.dot(c_g, st.astype(BF16), preferred_element_type=F32)
        y_ref[:, g * GW:(g + 1) * GW] = y_off * exp_a_e[:, g * GW:(g + 1) * GW]
        upd = lax.dot_general(b_g, xend_b[:, g * GW:(g + 1) * GW], (((0,), (0,)), ((), ())),
                              preferred_element_type=F32)
        state_ref[g] = st * exp_a_e[L - 1:L, g * GW:(g + 1) * GW] + upd
        for r in range(R):
            h = g * R + r
            seg = acum[:, h:h + 1] - acum_t[h:h + 1, :]
            decay = jnp.where(causal, jnp.exp(jnp.minimum(seg, 0.0)), 0.0)
            m = (cb * decay).astype(BF16)
            y_h = jnp.dot(m, xdt_b[:, h * P:(h + 1) * P], preferred_element_type=F32)
            y_ref[:, h * P:(h + 1) * P] += y_h

    y = y_ref[...] + xs * dsk_ref[...]
    yg = y * _silu(z_ref[0].astype(F32))
    for g in range(G):
        blk = yg[:, g * GW:(g + 1) * GW]
        ms = jnp.mean(blk * blk, axis=-1, keepdims=True)
        o_ref[0, :, g * GW:(g + 1) * GW] = (
            blk * lax.rsqrt(ms + NORM_EPS) * ng_ref[:, g * GW:(g + 1) * GW]).astype(o_ref.dtype)


def _ssd(proj, small, conv_w, conv_b, dt_bias, a_log, d_skip, norm_g, d_inner):
    bsz, s, _ = proj.shape
    L = SSM_CHUNK
    G, N, P = SSM_N_GROUPS, SSM_D_STATE, SSM_HEAD_DIM
    H = d_inner // P
    cdim = d_inner + 2 * G * N
    gw = d_inner // G
    zb = d_inner // d_inner
    bc0 = 2 * d_inner // (G * N)
    e_mat = (jnp.arange(d_inner, dtype=I32)[None, :] // P == jnp.arange(H, dtype=I32)[:, None]).astype(BF16)
    return pl.pallas_call(
        _ssd_kernel, name="ssd",
        grid=(bsz, s // L),
        in_specs=[pl.BlockSpec((1, L, d_inner), lambda b, c: (b, c, 0)),
                  pl.BlockSpec((1, L, d_inner), lambda b, c: (b, c, zb)),
                  pl.BlockSpec((1, L, G * N), lambda b, c: (b, c, bc0)),
                  pl.BlockSpec((1, L, G * N), lambda b, c: (b, c, bc0 + 1)),
                  pl.BlockSpec((1, L, 128), lambda b, c: (b, c, 2)),
                  pl.BlockSpec((SSM_CONV, cdim), lambda b, c: (0, 0)),
                  pl.BlockSpec((1, cdim), lambda b, c: (0, 0)),
                  pl.BlockSpec((1, H), lambda b, c: (0, 0)),
                  pl.BlockSpec((1, H), lambda b, c: (0, 0)),
                  pl.BlockSpec((1, d_inner), lambda b, c: (0, 0)),
                  pl.BlockSpec((1, d_inner), lambda b, c: (0, 0)),
                  pl.BlockSpec((H, d_inner), lambda b, c: (0, 0))],
        out_specs=pl.BlockSpec((1, L, d_inner), lambda b, c: (b, c, 0)),
        out_shape=jax.ShapeDtypeStruct((bsz, s, d_inner), BF16),
        scratch_shapes=[pltpu.VMEM((L + 8, cdim), F32),
                        pltpu.VMEM((G, N, gw), F32),
                        pltpu.VMEM((L, d_inner), F32)],
        compiler_params=_cparams(("arbitrary", "arbitrary")),
    )(proj, proj, proj, proj, small, conv_w, conv_b.reshape(1, cdim), dt_bias.reshape(1, H),
      a_log.reshape(1, H), jnp.repeat(d_skip, P).reshape(1, d_inner), norm_g.reshape(1, d_inner), e_mat)


def _pack_w_in(w_in):
    d = w_in.shape[0]
    d_inner = 2 * d
    gn = SSM_N_GROUPS * SSM_D_STATE
    n_h = d_inner // SSM_HEAD_DIM
    sizes = (d_inner, d_inner + 2 * gn, n_h, ATT_N_HEADS * ATT_HEAD_DIM, ATT_N_KV_HEADS * ATT_HEAD_DIM,
             ATT_N_KV_HEADS * ATT_HEAD_DIM, IDX_N_HEADS * IDX_HEAD_DIM, IDX_HEAD_DIM, IDX_N_HEADS, d, d)
    pts = [int(p) for p in np.cumsum(sizes)[:-1]]
    z, xbc, dt, q, k, v, qi, ki, wi, g_ssm, g_att = jnp.split(w_in, pts, axis=1)
    big = jnp.concatenate([z, xbc, q, g_ssm, g_att, k, v, qi], axis=1).astype(BF16)
    zeros = lambda n: jnp.zeros((d, n), w_in.dtype)
    small = jnp.concatenate([ki, zeros(128 - IDX_HEAD_DIM), wi, zeros(128 - IDX_N_HEADS), dt, zeros(128 - n_h)],
                            axis=1).astype(BF16)
    return big, small


def _rope_kernel(q_ref, k_ref, v_ref, qi_ref, sm_ref, wi_ref, ca_ref, sa_ref, ci_ref, si_ref,
                 qo_ref, ko_ref, vo_ref, qio_ref, kio_ref, wo_ref):
    ca = ca_ref[0]
    sa = sa_ref[0]
    ci = ci_ref[0]
    si = si_ref[0]
    att_scale = ATT_HEAD_DIM ** -0.5 * 1.4426950408889634
    q = q_ref[0].astype(F32)
    for h in range(ATT_N_HEADS):
        x = q[:, h * 128:(h + 1) * 128]
        qo_ref[0, h] = ((x * ca + pltpu.roll(x, 64, 1) * sa) * att_scale).astype(qo_ref.dtype)
    k = k_ref[0].astype(F32)
    for h in range(ATT_N_KV_HEADS):
        x = k[:, h * 128:(h + 1) * 128]
        ko_ref[0, :, h * 128:(h + 1) * 128] = (x * ca + pltpu.roll(x, 64, 1) * sa).astype(ko_ref.dtype)
    lane = lax.broadcasted_iota(I32, ci.shape, 1)
    first_half = (lane % IDX_HEAD_DIM) < (IDX_HEAD_DIM // 2)

    def rot_i(x):
        sw = jnp.where(first_half, pltpu.roll(x, 128 - IDX_HEAD_DIM // 2, 1), pltpu.roll(x, IDX_HEAD_DIM // 2, 1))
        return x * ci + sw * si

    qi = qi_ref[0].astype(F32)
    for j in range(IDX_N_HEADS * IDX_HEAD_DIM // 128):
        r = rot_i(qi[:, j * 128:(j + 1) * 128]).astype(qio_ref.dtype)
        for g in range(r.shape[0] // DSA_QG):
            qio_ref[0, g, j] = r[g * DSA_QG:(g + 1) * DSA_QG]
    ki = rot_i(sm_ref[0])
    kio_ref[0, :, 0:128] = ki.astype(kio_ref.dtype)
    kio_ref[0, :, 128:256] = pltpu.roll(ki, IDX_HEAD_DIM, 1).astype(kio_ref.dtype)
    wo_ref[0] = wi_ref[0] * (IDX_HEAD_DIM ** -0.5 * IDX_N_HEADS ** -0.5)
    v = v_ref[0]
    ones = jnp.ones((v.shape[0], 128), vo_ref.dtype)
    for h in range(ATT_N_KV_HEADS):
        vo_ref[0, :, h * 256:h * 256 + 128] = v[:, h * 128:(h + 1) * 128]
        vo_ref[0, :, h * 256 + 128:(h + 1) * 256] = ones


def _rope(proj, small, positions, d_model, tm):
    bsz, s, _ = proj.shape
    d_inner = 2 * d_model
    nq = ATT_N_HEADS * ATT_HEAD_DIM
    nkv = ATT_N_KV_HEADS * ATT_HEAD_DIM
    nqi = IDX_N_HEADS * IDX_HEAD_DIM
    q_off = 2 * d_inner + 2 * SSM_N_GROUPS * SSM_D_STATE
    k_off = q_off + nq + 2 * d_model
    qi_off = k_off + 2 * nkv

    def tables(dim):
        inv = ROPE_THETA ** (-jnp.arange(0, dim, 2, dtype=F32) / dim)
        ang = positions.astype(F32)[..., None] * inv
        return jnp.cos(ang), jnp.sin(ang)

    cos_a, sin_a = tables(ATT_HEAD_DIM)
    cos_i, sin_i = tables(IDX_HEAD_DIM)
    ca = jnp.concatenate([cos_a, cos_a], -1)
    sa = jnp.concatenate([-sin_a, sin_a], -1)
    ci = jnp.concatenate([cos_i] * 4, -1)
    si = jnp.concatenate([-sin_i, sin_i, -sin_i, sin_i], -1)
    tab = pl.BlockSpec((1, tm, 128), lambda b, i: (b, i, 0))
    return pl.pallas_call(
        _rope_kernel, name="rope",
        grid=(bsz, s // tm),
        in_specs=[pl.BlockSpec((1, tm, nq), lambda b, i: (b, i, q_off // nq)),
                  pl.BlockSpec((1, tm, nkv), lambda b, i: (b, i, k_off // nkv)),
                  pl.BlockSpec((1, tm, nkv), lambda b, i: (b, i, k_off // nkv + 1)),
                  pl.BlockSpec((1, tm, nqi), lambda b, i: (b, i, qi_off // nqi)),
                  pl.BlockSpec((1, tm, 128), lambda b, i: (b, i, 0)),
                  pl.BlockSpec((1, tm, 128), lambda b, i: (b, i, 1)),
                  tab, tab, tab, tab],
        out_specs=[pl.BlockSpec((1, ATT_N_HEADS, tm, 128), lambda b, i: (b, 0, i, 0)),
                   pl.BlockSpec((1, tm, nkv), lambda b, i: (b, i, 0)),
                   pl.BlockSpec((1, tm, 2 * nkv), lambda b, i: (b, i, 0)),
                   pl.BlockSpec((1, tm // DSA_QG, nqi // 128, DSA_QG, 128), lambda b, i: (b, i, 0, 0, 0)),
                   pl.BlockSpec((1, tm, 256), lambda b, i: (b, i, 0)),
                   tab],
        out_shape=[jax.ShapeDtypeStruct((bsz, ATT_N_HEADS, s, 128), BF16),
                   jax.ShapeDtypeStruct((bsz, s, nkv), BF16),
                   jax.ShapeDtypeStruct((bsz, s, 2 * nkv), BF16),
                   jax.ShapeDtypeStruct((bsz, s // DSA_QG, nqi // 128, DSA_QG, 128), BF16),
                   jax.ShapeDtypeStruct((bsz, s, 256), BF16),
                   jax.ShapeDtypeStruct((bsz, s, 128), F32)],
        compiler_params=_cparams(("arbitrary", "arbitrary")),
    )(proj, proj, proj, proj, small, small, ca, sa, ci, si)


DSA_TQ_SEL = 256
DSA_TQ = 256
DSA_TK = 512
DSA_TK_ATT = 512
DSA_RC = 64
DSA_QG = 16
DSA_FILL_TILES = 4


def _fill_head_weights(wb_ref, w):
    tq = w.shape[0]
    for h in range(IDX_N_HEADS):
        wb_ref[h] = jnp.broadcast_to(w[:, h:h + 1], (tq, 128))


def _score_keys(qi_groups, wb_ref, k2, emit):
    n_grp, n_pair, qg, _ = qi_groups.shape
    lhs = qi_groups.reshape(n_grp * n_pair * qg, 128)
    nt = (((1,), (1,)), ((), ()))
    l_even = lax.dot_general(lhs, k2[:, 0:128], nt, preferred_element_type=F32)
    l_odd = lax.dot_general(lhs, k2[:, 128:256], nt, preferred_element_type=F32)
    for g in range(n_grp):
        for c in range(k2.shape[0] // 128):
            sc = None
            for j in range(n_pair):
                r0 = (g * n_pair + j) * qg
                for l, h in ((l_even, 2 * j), (l_odd, 2 * j + 1)):
                    t = jnp.maximum(l[r0:r0 + qg, c * 128:(c + 1) * 128], 0.0) * wb_ref[h, g * qg:(g + 1) * qg, :]
                    sc = t if sc is None else sc + t
            bits = lax.bitcast_convert_type(sc, I32)
            bits = jnp.where(bits == INT_MIN, 0, bits)
            emit(g, c, bits ^ ((bits >> 31) & 0x7FFFFFFF))


def _select_kernel(qi_ref, w_ref, ki_ref, mask_hbm, keys_ref, h16_ref, wb_ref, cut_ref, stage_ref, sem,
                   *, topk, seq):
    tq, tk = DSA_TQ_SEL, DSA_TK
    nch = tk // 128
    nt = seq // tk
    b = pl.program_id(0)
    i = pl.program_id(1)
    nkt = (i * tq + tq - 1) // tk + 1
    qi = qi_ref[0]
    _fill_head_weights(wb_ref, w_ref[0])
    qpos = i * tq + lax.broadcasted_iota(I32, (tq, 128), 0)
    lane = lax.broadcasted_iota(I32, (tq, 128), 1)
    chunk = lambda c: slice(c * 128, (c + 1) * 128)

    tpi = DSA_FILL_TILES if nt % DSA_FILL_TILES == 0 else 1

    def fill(it, carry):
        for u in range(tpi):
            t = it * tpi + u

            def emit(g, c, key, t=t):
                rows = slice(g * DSA_QG, (g + 1) * DSA_QG)
                kpos = t * tk + c * 128 + lax.broadcasted_iota(I32, key.shape, 1)
                key = jnp.where(kpos <= i * tq + g * DSA_QG + lax.broadcasted_iota(I32, key.shape, 0), key, INT_MIN)
                keys_ref[t, rows, chunk(c)] = key
                h16_ref[t, rows, chunk(c)] = (key >> 16).astype(I16)
            _score_keys(qi, wb_ref, ki_ref[0, pl.ds(pl.multiple_of(t * tk, tk), tk), :], emit)
        return carry

    nit = (nkt + tpi - 1) // tpi
    lax.fori_loop(0, nit, fill, 0)
    kf = float(topk)

    def tile_loop(body, init):
        def group(it, carry):
            for u in range(tpi):
                carry = body(it * tpi + u, carry)
            return carry
        return lax.fori_loop(0, nit, group, init)

    def search16():
        def count16(cand):
            def body(t, acc):
                for c in range(nch):
                    acc = acc + jnp.where(h16_ref[t, :, chunk(c)] >= cand, jnp.int16(1), jnp.int16(0))
                return acc
            acc = tile_loop(body, jnp.zeros((tq, 128), I16))
            return jnp.sum(acc.astype(F32), axis=-1, keepdims=True)

        v = jnp.where(count16(jnp.zeros((tq, 128), I16)) >= kf, 0, -32768).astype(I32)

        def bit_body(bi, v):
            cand = v | jnp.left_shift(jnp.int32(1), 14 - bi)
            c = count16(jnp.broadcast_to(cand, (tq, 128)).astype(I16))
            return jnp.where(c >= kf, cand, v)
        return lax.fori_loop(0, 15, bit_body, v)

    thr_hi = search16()
    hi_b = jnp.broadcast_to(thr_hi, (tq, 128))

    def lower_half(t, carry):
        for c in range(nch):
            k = keys_ref[t, :, chunk(c)]
            hi = k >> 16
            lo = (k & 0xFFFF) - 32768
            h16_ref[t, :, chunk(c)] = jnp.where(hi == hi_b, lo, jnp.where(hi > hi_b, 32767, -32768)).astype(I16)
        return carry

    tile_loop(lower_half, 0)
    thr =jnp.left_shift(thr_hi, 16) + (search16() + 32768)
    thr_b = jnp.broadcast_to(thr, (tq, 128))

    def count(pred):
        def body(t, acc):
            for c in range(nch):
                acc = acc + jnp.where(pred(keys_ref[t, :, chunk(c)], t * tk + c * 128), 1.0, 0.0)
            return acc
        acc = tile_loop(body, jnp.zeros((tq, 128), F32))
        return jnp.sum(acc, axis=-1, keepdims=True)

    need = kf - count(lambda k, base: k > thr_b)
    n_eq = count(lambda k, base: k == thr_b)
    cut_ref[...] = jnp.full((tq, 128), seq, I32)
    partial = jnp.logical_and(n_eq > need, thr > INT_MIN)

    @pl.when(jnp.max(jnp.where(partial, 1.0, 0.0)) > 0.0)
    def _():
        def idx_body(bi, cut):
            cand = cut | jnp.left_shift(jnp.int32(1), (seq.bit_length() - 1) - bi)
            cand_b = jnp.broadcast_to(cand, (tq, 128))
            c = count(lambda k, base: jnp.logical_and(k == thr_b, base + lane < cand_b))
            return jnp.where(c < need, cand, cut)
        cut = lax.fori_loop(0, seq.bit_length(), idx_body, jnp.zeros((tq, 1), I32))
        cut_ref[...] = jnp.broadcast_to(cut, (tq, 128))

    cut_b = jnp.where(thr_b == INT_MIN, qpos, cut_ref[...])

    def tile_copy(t):
        return pltpu.make_async_copy(stage_ref.at[t], mask_hbm.at[b, i, t], sem.at[0])

    @pl.when(jnp.logical_or(b > 0, i > 0))
    def _():
        for t in range(nt):
            tile_copy(t).wait()

    def mask_tile(t, carry):
        for c in range(nch):
            k = keys_ref[t, :, chunk(c)]
            sel = jnp.logical_or(k > thr_b, jnp.logical_and(k == thr_b, t * tk + c * 128 + lane <= cut_b))
            stage_ref[t, :, chunk(c)] = jnp.where(sel, 0.0, NEG_BIG).astype(stage_ref.dtype)
        return carry

    tile_loop(mask_tile, 0)

    def dead_tile(t, carry):
        stage_ref[t] = jnp.full((tq, tk), NEG_BIG, stage_ref.dtype)
        return carry

    lax.fori_loop(nit * tpi, nt, dead_tile, 0)
    for t in range(nt):
        tile_copy(t).start()

    @pl.when(jnp.logical_and(b == pl.num_programs(0) - 1, i == pl.num_programs(1) - 1))
    def _():
        for t in range(nt):
            tile_copy(t).wait()


def _select(qi_rot, w_s, ki_rot, topk):
    bsz, n_grp, n_pair, qg, _ = qi_rot.shape
    s = n_grp * qg
    tq, tk = DSA_TQ_SEL, DSA_TK
    return pl.pallas_call(
        functools.partial(_select_kernel, topk=topk, seq=s), name="dsa_select",
        grid=(bsz, s // tq),
        in_specs=[pl.BlockSpec((1, tq // qg, n_pair, qg, 128), lambda b, i: (b, i, 0, 0, 0)),
                  pl.BlockSpec((1, tq, 128), lambda b, i: (b, i, 0)),
                  pl.BlockSpec((1, s, 256), lambda b, i: (b, 0, 0))],
        out_specs=pl.BlockSpec(memory_space=pl.ANY),
        out_shape=jax.ShapeDtypeStruct((bsz, s // tq, s // tk, tq, tk), BF16),
        scratch_shapes=[pltpu.VMEM((s // tk, tq, tk), I32),
                        pltpu.VMEM((s // tk, tq, tk), I16),
                        pltpu.VMEM((IDX_N_HEADS, tq, 128), F32),
                        pltpu.VMEM((tq, 128), I32),
                        pltpu.VMEM((s // tk, tq, tk), BF16),
                        pltpu.SemaphoreType.DMA((1,))],
        compiler_params=_cparams(("arbitrary", "arbitrary")),
    )(qi_rot, w_s, ki_rot)


def _attn_kernel(qb_ref, kt_ref, q_ref, k_ref, v_ref, mask_ref, o_ref, m_ref, acc_ref, bias_ref, s_ref, p_ref, al_ref):
    tq, tk, rc = DSA_TQ, DSA_TK_ATT, DSA_RC
    rep = ATT_N_HEADS // ATT_N_KV_HEADS
    nch = tk // 128
    step = pl.program_id(1)
    qb = qb_ref[step]
    kt = kt_ref[step]

    @pl.when(kt == 0)
    def _():
        m_ref[...] = jnp.full(m_ref.shape, NEG_BIG, F32)
        acc_ref[...] = jnp.zeros(acc_ref.shape, F32)

    for r in range(tq // DSA_TQ_SEL):
        for c in range(tk // DSA_TK):
            bias_ref[r * DSA_TQ_SEL:(r + 1) * DSA_TQ_SEL, c * DSA_TK:(c + 1) * DSA_TK] = mask_ref[0, r, c].astype(F32)

    for g in range(ATT_N_KV_HEADS):
        qg = q_ref[0, g * rep:(g + 1) * rep].reshape(rep * tq, ATT_HEAD_DIM)
        kg = k_ref[0, :, g * 128:(g + 1) * 128]
        s_ref[...] = lax.dot_general(qg, kg, (((1,), (1,)), ((), ())), preferred_element_type=F32)

        def chunk(ci, carry):
            rows = pl.ds(pl.multiple_of(ci * rc, rc), rc)
            brows = pl.ds(pl.multiple_of((ci % (tq // rc)) * rc, rc), rc)
            sc = [s_ref[rows, c * 128:(c + 1) * 128] + bias_ref[brows, c * 128:(c + 1) * 128] for c in range(nch)]
            mx = sc[0]
            for c in range(1, nch):
                mx = jnp.maximum(mx, sc[c])
            m_old = m_ref[g, rows, :]
            m_new = jnp.maximum(m_old, jnp.max(mx, axis=-1, keepdims=True))
            al_ref[rows, :] = jnp.exp2(m_old - m_new)
            m_ref[g, rows, :] = m_new
            for c in range(nch):
                p_ref[rows, c * 128:(c + 1) * 128] = jnp.exp2(sc[c] - m_new).astype(p_ref.dtype)
            return carry

        lax.fori_loop(0, rep * tq // rc, chunk, 0, unroll=True)
        pv = jnp.dot(p_ref[...], v_ref[0, :, g * 256:(g + 1) * 256], preferred_element_type=F32)
        al = al_ref[...]
        acc_ref[g, :, 0:128] = al * acc_ref[g, :, 0:128] + pv[:, 0:128]
        acc_ref[g, :, 128:256] = al * acc_ref[g, :, 128:256] + pv[:, 128:256]

    @pl.when(kt == (qb * tq + tq - 1) // tk)
    def _():
        for g in range(ATT_N_KV_HEADS):
            o = acc_ref[g, :, 0:128] / acc_ref[g, :, 128:256]
            for r in range(rep):
                h = g * rep + r
                o_ref[0, :, h * 128:(h + 1) * 128] = o[r * tq:(r + 1) * tq].astype(o_ref.dtype)


def _attn(q_rot, k_rot, v_ext, mask):
    bsz, n_h, s, _ = q_rot.shape
    tq, tk = DSA_TQ, DSA_TK_ATT
    nkv = ATT_N_KV_HEADS * ATT_HEAD_DIM
    pairs = [(qb, kt) for qb in range(s // tq) for kt in range((qb * tq + tq - 1) // tk + 1)]
    qb_arr = jnp.asarray([p[0] for p in pairs], I32)
    kt_arr = jnp.asarray([p[1] for p in pairs], I32)
    rep = n_h // ATT_N_KV_HEADS
    grid_spec = pltpu.PrefetchScalarGridSpec(
        num_scalar_prefetch=2,
        grid=(bsz, len(pairs)),
        in_specs=[pl.BlockSpec((1, n_h, tq, 128), lambda b, t, qb, kt: (b, 0, qb[t], 0)),
                  pl.BlockSpec((1, tk, nkv), lambda b, t, qb, kt: (b, kt[t], 0)),
                  pl.BlockSpec((1, tk, 2 * nkv), lambda b, t, qb, kt: (b, kt[t], 0)),
                  pl.BlockSpec((1, tq // DSA_TQ_SEL, tk // DSA_TK, DSA_TQ_SEL, DSA_TK),
                               lambda b, t, qb, kt: (b, qb[t], kt[t], 0, 0))],
        out_specs=pl.BlockSpec((1, tq, n_h * 128), lambda b, t, qb, kt: (b, qb[t], 0)),
        scratch_shapes=[pltpu.VMEM((ATT_N_KV_HEADS, rep * tq, 128), F32),
                        pltpu.VMEM((ATT_N_KV_HEADS, rep * tq, 256), F32),
                        pltpu.VMEM((tq, tk), F32),
                        pltpu.VMEM((rep * tq, tk), F32),
                        pltpu.VMEM((rep * tq, tk), BF16),
                        pltpu.VMEM((rep * tq, 128), F32)])
    return pl.pallas_call(
        _attn_kernel, name="dsa_attn",
        grid_spec=grid_spec,
        out_shape=jax.ShapeDtypeStruct((bsz, s, n_h * 128), BF16),
        compiler_params=_cparams(("arbitrary", "arbitrary")),
    )(qb_arr, kt_arr, q_rot, k_rot, v_ext, mask)


def _merge_kernel(y_ref, o_ref, gs_ref, ga_ref, ws_ref, wa_ref, out_ref):
    a = jnp.dot(y_ref[0], ws_ref[...], preferred_element_type=F32)
    b = jnp.dot(o_ref[0], wa_ref[...], preferred_element_type=F32)
    out_ref[0] = (_sigmoid(gs_ref[0].astype(F32)) * a + _sigmoid(ga_ref[0].astype(F32)) * b).astype(out_ref.dtype)


def _merge(y_ssm, o_att, proj, w_ssm, w_att, tm, tn):
    bsz, s, d_inner = y_ssm.shape
    d_att = o_att.shape[2]
    d = w_ssm.shape[1]
    gs_off = 2 * d_inner + 2 * SSM_N_GROUPS * SSM_D_STATE + d_att
    ga_off = gs_off + d
    return pl.pallas_call(
        _merge_kernel, name="merge",
        grid=(bsz, s // tm, d // tn),
        in_specs=[pl.BlockSpec((1, tm, d_inner), lambda b, i, j: (b, i, 0)),
                  pl.BlockSpec((1, tm, d_att), lambda b, i, j: (b, i, 0)),
                  pl.BlockSpec((1, tm, tn), lambda b, i, j: (b, i, gs_off // tn + j)),
                  pl.BlockSpec((1, tm, tn), lambda b, i, j: (b, i, ga_off // tn + j)),
                  pl.BlockSpec((d_inner, tn), lambda b, i, j: (0, j)),
                  pl.BlockSpec((d_att, tn), lambda b, i, j: (0, j))],
        out_specs=pl.BlockSpec((1, tm, tn), lambda b, i, j: (b, i, j)),
        out_shape=jax.ShapeDtypeStruct((bsz, s, d), BF16),
        compiler_params=_cparams(("arbitrary", "arbitrary", "arbitrary")),
    )(y_ssm, o_att, proj, proj, w_ssm, w_att)


def _post_kernel(m_ref, wo_ref, x_ref, mod_ref, g_ref, rw_ref, rb_ref, x1_ref, h2_ref, gate_ref, idx_ref):
    mod = mod_ref[0]
    out = jnp.dot(m_ref[0], wo_ref[...], preferred_element_type=F32)
    x1 = x_ref[0] + mod[2:3] * out
    x1_ref[0] = x1
    h2 = _modnorm(x1, g_ref[...], mod[4:5], mod[3:4])
    h2_ref[0] = h2
    logit = _dot_hi(h2, rw_ref[...]) + rb_ref[...]
    lane = lax.broadcasted_iota(I32, logit.shape, 1)
    lane_f = lane.astype(F32)
    vals, idxs = [], []
    for _ in range(TOP_K):
        m = jnp.max(logit, axis=-1, keepdims=True)
        idx = jnp.min(jnp.where(logit == m, lane_f, 128.0), axis=-1, keepdims=True)
        vals.append(m)
        idxs.append(idx)
        logit = jnp.where(lane_f == idx, -3e38, logit)
    es = [jnp.exp(v - vals[0]) for v in vals]
    denom = es[0] + es[1] + es[2] + es[3]
    gate = jnp.zeros(logit.shape, F32)
    sel = jnp.zeros(logit.shape, F32)
    for k in range(TOP_K):
        gate = jnp.where(lane == k, es[k] / denom, gate)
        sel = jnp.where(lane == k, idxs[k], sel)
    gate_ref[0] = gate
    idx_ref[0] = sel.astype(I32)


def _post(merged, w_o, x, mod, g, router_w, router_b, tm):
    bsz, s, d = x.shape
    n_e = router_w.shape[1]
    rw = jnp.zeros((d, 128), F32).at[:, :n_e].set(router_w.astype(F32))
    rb = jnp.full((1, 128), NEG_BIG, F32).at[0, :n_e].set(router_b.astype(F32))
    row = lambda w: pl.BlockSpec((1, tm, w), lambda b, i: (b, i, 0))
    return pl.pallas_call(
        _post_kernel, name="post_router",
        grid=(bsz, s // tm),
        in_specs=[row(d),
                  pl.BlockSpec((d, d), lambda b, i: (0, 0)),
                  row(d),
                  pl.BlockSpec((1, 6, d), lambda b, i: (b, 0, 0)),
                  pl.BlockSpec((1, d), lambda b, i: (0, 0)),
                  pl.BlockSpec((d, 128), lambda b, i: (0, 0)),
                  pl.BlockSpec((1, 128), lambda b, i: (0, 0))],
        out_specs=[row(d), row(d), row(128), row(128)],
        out_shape=[jax.ShapeDtypeStruct((bsz, s, d), F32),
                   jax.ShapeDtypeStruct((bsz, s, d), F32),
                   jax.ShapeDtypeStruct((bsz, s, 128), F32),
                   jax.ShapeDtypeStruct((bsz, s, 128), I32)],
        compiler_params=_cparams(("arbitrary", "arbitrary")),
    )(merged, w_o, x, mod, g.reshape(1, d), rw, rb)


MOE_TM = 512
MOE_TN = 512


def _moe_kernel(be_ref, nu_ref, rt_ref, rtn_ref, rd_ref, h_hbm, wg_ref, wu_ref, bg_ref, bu_ref, wd_ref, bd_ref,
                y_hbm, xf_ref, xb_ref, acc_ref, yo_ref, sem):
    tm = MOE_TM
    blk = pl.program_id(0)
    j = pl.program_id(1)
    n_j = pl.num_programs(1)
    n_used = nu_ref[0]

    def hbm_row(ref, r):
        return ref.at[lax.shift_right_logical(r, 3), pl.ds(r & 7, 1)]

    def gather_start(tok_ref):
        def body(o, c):
            for u in range(8):
                pltpu.make_async_copy(hbm_row(h_hbm, tok_ref[0, 0, o * 8 + u]), xf_ref.at[o, pl.ds(u, 1)],
                                      sem.at[0]).start()
            return c
        lax.fori_loop(0, tm // 8, body, 0)

    def all_rows_in():
        return pltpu.make_async_copy(h_hbm.at[pl.ds(0, tm // 8)], xf_ref, sem.at[0])

    def all_rows_out():
        return pltpu.make_async_copy(yo_ref, y_hbm.at[pl.ds(0, tm // 8)], sem.at[1])

    @pl.when(blk < n_used)
    def _():
        @pl.when(j == 0)
        def _():
            @pl.when(blk == 0)
            def _():
                gather_start(rt_ref)
            all_rows_in().wait()
            xb_ref[...] = xf_ref[...].reshape(xb_ref.shape).astype(BF16)
            acc_ref[...] = jnp.zeros(acc_ref.shape, F32)

            @pl.when(blk + 1 < n_used)
            def _():
                gather_start(rtn_ref)

        x = xb_ref[...]
        gate = jnp.dot(x, wg_ref[0], preferred_element_type=F32) + bg_ref[0]
        up = jnp.dot(x, wu_ref[0], preferred_element_type=F32) + bu_ref[0]
        gate = jnp.minimum(gate, SWIGLU_LIMIT)
        up = jnp.clip(up, -SWIGLU_LIMIT, SWIGLU_LIMIT)
        act = (up + 1.0) * (gate * _sigmoid(SWIGLU_ALPHA * gate))
        acc_ref[...] += jnp.dot(act.astype(BF16), wd_ref[0], preferred_element_type=F32)

        @pl.when(j == n_j - 1)
        def _():
            @pl.when(blk > 0)
            def _():
                all_rows_out().wait()

            yo_ref[...] = (acc_ref[...] + bd_ref[0]).reshape(yo_ref.shape)

            def body(o, c):
                for u in range(8):
                    pltpu.make_async_copy(yo_ref.at[o, pl.ds(u, 1)], hbm_row(y_hbm, rd_ref[0, 0, o * 8 + u]),
                                          sem.at[1]).start()
                return c
            lax.fori_loop(0, tm // 8, body, 0)

            @pl.when(blk == n_used - 1)
            def _():
                all_rows_out().wait()

    @pl.when(jnp.logical_and(blk >= n_used, j == 0))
    def _():
        @pl.when(blk == n_used)
        def _():
            yo_ref[...] = jnp.zeros(yo_ref.shape, F32)
        fill = pltpu.make_async_copy(yo_ref, y_hbm.at[pl.ds(blk * (tm // 8), tm // 8)], sem.at[1])
        fill.start()
        fill.wait()


def _moe(h2, gate_idx, w_gu, b_gu, w_dn, b_dn):
    n_tok, d = h2.shape
    n_e, _, d2 = w_gu.shape
    d_e = d2 // 2
    tm, tn = MOE_TM, MOE_TN
    n_j = d_e // tn
    n_asg = n_tok * TOP_K
    e_flat = gate_idx[:, :TOP_K].reshape(-1)
    onehot = (e_flat[:, None] == jnp.arange(n_e, dtype=I32)[None, :]).astype(I32)
    csum = jnp.cumsum(onehot, axis=0)
    rank = jnp.take_along_axis(csum, e_flat[:, None], axis=1)[:, 0] - 1
    counts = csum[-1]
    padded = (counts + tm - 1) // tm * tm
    ends = jnp.cumsum(padded)
    dest = (ends - padded)[e_flat] + rank
    n_blocks = -(-(n_asg + n_e * (tm - 1)) // tm)
    row_asg = jnp.full((n_blocks * tm,), -1, I32).at[dest].set(jnp.arange(n_asg, dtype=I32))
    is_pad = row_asg < 0
    pad_rank = jnp.cumsum(is_pad.astype(I32)) - 1
    row_tok = jnp.where(is_pad, 0, row_asg // TOP_K).reshape(n_blocks, 1, tm)
    row_dst = jnp.where(is_pad, n_asg + pad_rank, (row_asg % TOP_K) * n_tok + row_asg // TOP_K)
    row_dst = row_dst.reshape(n_blocks, 1, tm)
    block_expert = jnp.minimum(
        jnp.searchsorted(ends, jnp.arange(n_blocks, dtype=I32) * tm, side='right'), n_e - 1).astype(I32)
    n_used = (ends[-1:] // tm).astype(I32)

    def jj(blk, j, nu):
        return jnp.where(blk < nu[0], j, n_j - 1)

    smem_rows = pl.BlockSpec((1, 1, tm), lambda blk, j, be, nu: (blk, 0, 0), memory_space=pltpu.SMEM)
    smem_next = pl.BlockSpec((1, 1, tm), lambda blk, j, be, nu: (jnp.minimum(blk + 1, n_blocks - 1), 0, 0),
                             memory_space=pltpu.SMEM)
    grid_spec = pltpu.PrefetchScalarGridSpec(
        num_scalar_prefetch=2,
        grid=(n_blocks, n_j),
        in_specs=[smem_rows, smem_next, smem_rows,
                  pl.BlockSpec(memory_space=pl.ANY),
                  pl.BlockSpec((1, d, tn), lambda blk, j, be, nu: (be[blk], 0, jj(blk, j, nu))),
                  pl.BlockSpec((1, d, tn), lambda blk, j, be, nu: (be[blk], 0, n_j + jj(blk, j, nu))),
                  pl.BlockSpec((1, 1, tn), lambda blk, j, be, nu: (be[blk], 0, jj(blk, j, nu))),
                  pl.BlockSpec((1, 1, tn), lambda blk, j, be, nu: (be[blk], 0, n_j + jj(blk, j, nu))),
                  pl.BlockSpec((1, tn, d), lambda blk, j, be, nu: (be[blk], jj(blk, j, nu), 0)),
                  pl.BlockSpec((1, 1, d), lambda blk, j, be, nu: (be[blk], 0, 0))],
        out_specs=pl.BlockSpec(memory_space=pl.ANY),
        scratch_shapes=[pltpu.VMEM((tm // 8, 8, d), F32),
                        pltpu.VMEM((tm, d), BF16),
                        pltpu.VMEM((tm, d), F32),
                        pltpu.VMEM((tm // 8, 8, d), F32),
                        pltpu.SemaphoreType.DMA((2,))])
    y = pl.pallas_call(
        _moe_kernel, name="moe_experts",
        grid_spec=grid_spec,
        out_shape=jax.ShapeDtypeStruct((n_blocks * tm // 8, 8, d), F32),
        compiler_params=_cparams(("arbitrary", "arbitrary")),
    )(block_expert, n_used, row_tok, row_tok, row_dst, h2.reshape(n_tok // 8, 8, d), w_gu, w_gu,
      b_gu.reshape(n_e, 1, d2), b_gu.reshape(n_e, 1, d2), w_dn, b_dn.reshape(n_e, 1, d))
    return y.reshape(n_blocks * tm, d)


def _combine_kernel(x1_ref, y0_ref, y1_ref, y2_ref, y3_ref, gate_ref, mod_ref, g_ref, o_ref, *, final):
    gate = gate_ref[0]
    moe = gate[:, 0:1] * y0_ref[...]
    for k, y_ref in ((1, y1_ref), (2, y2_ref), (3, y3_ref)):
        moe = moe + gate[:, k:k + 1] * y_ref[...]
    x2 = x1_ref[0] + mod_ref[0][5:6] * moe
    if final:
        ms = jnp.mean(x2 * x2, axis=-1, keepdims=True)
        x2 = x2 * lax.rsqrt(ms + NORM_EPS) * g_ref[...]
    o_ref[0] = x2


def _combine(x1, y, gates, mod, g, final, tm):
    bsz, s, d = x1.shape
    assert TOP_K == 4
    row = lambda w: pl.BlockSpec((1, tm, w), lambda b, i: (b, i, 0))
    nb = s // tm

    def slot(k):
        return pl.BlockSpec((tm, d), lambda b, i: (k * bsz * nb + b * nb + i, 0))

    return pl.pallas_call(
        functools.partial(_combine_kernel, final=final), name="combine",
        grid=(bsz, nb),
        in_specs=[row(d), slot(0), slot(1), slot(2), slot(3), row(128),
                  pl.BlockSpec((1, 6, d), lambda b, i: (b, 0, 0)),
                  pl.BlockSpec((1, d), lambda b, i: (0, 0))],
        out_specs=row(d),
        out_shape=jax.ShapeDtypeStruct((bsz, s, d), F32),
        compiler_params=_cparams(("arbitrary", "arbitrary")),
    )(x1, y, y, y, y, gates, mod, g.reshape(1, d))


def _pick(n, *cands):
    for c in cands:
        if n % c == 0:
            return c
    return n


def kernel(x, c, positions, ada_w, ada_b, norm_mix_g, w_in, conv_w, conv_b, dt_bias, a_log, d_skip, ssm_norm_g,
           w_out_ssm, w_out_att, w_o, norm_ffn_g, router_w, router_b, expert_w_gate_up, expert_b_gate_up,
           expert_w_down, expert_b_down, final_norm_g):
    bsz, s, d = x.shape
    depth = ada_w.shape[0]
    topk = min(IDX_TOPK_MAX, s // 4)
    for l in range(depth):
        mod = _adaln(c, ada_w[l], ada_b[l]).reshape(bsz, 6, d)
        w_big, w_small = _pack_w_in(w_in[l])
        tm = _pick(s, 1024, 512, 256)
        proj = _inproj(x, norm_mix_g[l], mod, w_big, BF16, tm, 1280)
        small = _inproj(x, norm_mix_g[l], mod, w_small, F32, tm, w_small.shape[1])
        y_ssm = _ssd(proj, small, conv_w[l], conv_b[l], dt_bias[l], a_log[l], d_skip[l], ssm_norm_g[l], 2 * d)
        q_rot, k_rot, v_ext, qi_rot, ki_rot, w_s = _rope(proj, small, positions, d, _pick(s, 512, 256))
        mask = _select(qi_rot, w_s, ki_rot, topk)
        o_att = _attn(q_rot, k_rot, v_ext, mask)
        merged = _merge(y_ssm, o_att, proj, w_out_ssm[l].astype(BF16), w_out_att[l].astype(BF16),
                        _pick(s, 512, 256), 1024)
        x1, h2, gates, gate_idx = _post(merged, w_o[l].astype(BF16), x, mod, norm_ffn_g[l], router_w[l],
                                        router_b[l], 256)
        y = _moe(h2.reshape(bsz * s, d), gate_idx.reshape(bsz * s, 128), expert_w_gate_up[l].astype(BF16),
                 expert_b_gate_up[l], expert_w_down[l].astype(BF16), expert_b_down[l])
        x = _combine(x1, y, gates, mod, final_norm_g, l == depth - 1, 256)
    return x
```
